```python
import math
import jax
import jax.numpy as jnp
from jax import lax
import numpy as np

D_MODEL = 1024
BATCH = 8
SEQ = 4096
DEPTH = 2

HEAD_DIM = 64
ROT_DIM = HEAD_DIM // 4
ROPE_THETA = 500000.0
A_GROUPS = ((128, 1), (512, 4), (2048, 16))
A_HEADS = 8
B_HEADS = D_MODEL // (2 * HEAD_DIM)
D_FF = ((8 * D_MODEL // 3 + 255) // 256) * 256
CONV_WIDTH = 3
Q_BLOCK = 128
N_A_LAYERS = DEPTH // 2
N_B_LAYERS = DEPTH - N_A_LAYERS
NORM_EPS = 1e-6
SUBLN_EPS = 1e-5
NEG_INF = -1e30

kernel_name = "yoco_dilated_diffattn_convffn_block"


def rms_norm(x, g, eps=NORM_EPS):
    x32 = x.astype(jnp.float32)
    y = x32 * lax.rsqrt(jnp.mean(x32 * x32, axis=-1, keepdims=True) + eps)
    return (y * g.astype(jnp.float32)).astype(x.dtype)


def ada_mod(c_act, w, b, n):
    m = c_act @ w + b
    return jnp.split(m[:, None, :], n, axis=-1)


def rope_tables(positions):
    inv = jnp.power(ROPE_THETA, -jnp.arange(0, ROT_DIM, 2, dtype=jnp.float32) / ROT_DIM)
    ang = positions.astype(jnp.float32)[..., None] * inv
    return jnp.cos(ang), jnp.sin(ang)


def apply_partial_rope(x, cos, sin):
    extra = x.ndim - 3
    shp = cos.shape[:2] + (1,) * extra + cos.shape[-1:]
    cos = cos.reshape(shp).astype(x.dtype)
    sin = sin.reshape(shp).astype(x.dtype)
    half = ROT_DIM // 2
    x1 = x[..., :half]
    x2 = x[..., half:ROT_DIM]
    return jnp.concatenate([x1 * cos - x2 * sin, x2 * cos + x1 * sin, x[..., ROT_DIM:]], axis=-1)


def dilated_window_attention(q, k, v, window, dilation):
    B, S, H, Dh = q.shape
    L = S // dilation
    W = window // dilation
    nb = -(-L // W)
    Lp = nb * W

    def by_residue(t):
        return t.reshape(B, L, dilation, H, Dh).transpose(0, 2, 1, 3, 4)

    qr, kr, vr = by_residue(q), by_residue(k), by_residue(v)
    qb = jnp.pad(qr, ((0, 0), (0, 0), (0, Lp - L), (0, 0), (0, 0))).reshape(B, dilation, nb, W, H, Dh)

    def kv_blocks(t):
        tb = jnp.pad(t, ((0, 0), (0, 0), (W, Lp - L), (0, 0), (0, 0))).reshape(B, dilation, nb + 1, W, H, Dh)
        return jnp.concatenate([tb[:, :, :-1], tb[:, :, 1:]], axis=3)

    kb, vb = kv_blocks(kr), kv_blocks(vr)
    i = jnp.arange(W)[:, None]
    j = jnp.arange(2 * W)[None, :]
    n = jnp.arange(nb)[:, None, None]
    delta = W + i - j
    mask = (delta >= 0) & (delta <= W) & (n * W + j - W >= 0)

    s = jnp.einsum('brnqhd,brnkhd->brnhqk', qb, kb).astype(jnp.float32) * (Dh ** -0.5)
    s = jnp.where(mask[None, None, :, None], s, NEG_INF)
    mx = jnp.max(s, axis=-1, keepdims=True)
    p = jnp.exp(s - mx)
    den = jnp.sum(p, axis=-1)
    o = jnp.einsum('brnhqk,brnkhd->brnqhd', p.astype(v.dtype), vb).astype(jnp.float32)
    o = o / jnp.transpose(den, (0, 1, 2, 4, 3))[..., None]
    lse = jnp.transpose(mx[..., 0] + jnp.log(den), (0, 1, 2, 4, 3))

    def back(t):
        t = t.reshape((B, dilation, Lp) + t.shape[4:])[:, :, :L]
        t = jnp.moveaxis(t, 1, 2)
        return t.reshape((B, S) + t.shape[3:])

    return back(o), back(lse)


def dilated_mixer(h, w_qkv, w_o, cos, sin):
    B, S, _ = h.shape
    qkv = (h @ w_qkv).reshape(B, S, len(A_GROUPS), 3, A_HEADS, HEAD_DIM)
    outs, lses = [], []
    for g, (window, dil) in enumerate(A_GROUPS):
        q = apply_partial_rope(qkv[:, :, g, 0], cos, sin)
        k = apply_partial_rope(qkv[:, :, g, 1], cos, sin)
        o, lse = dilated_window_attention(q, k, qkv[:, :, g, 2], window, dil)
        outs.append(o)
        lses.append(lse)
    o = jnp.stack(outs, axis=0)
    alpha = jax.nn.softmax(jnp.stack(lses, axis=0), axis=0)
    mixed = jnp.sum(alpha[..., None] * o, axis=0)
    return mixed.reshape(B, S, A_HEADS * HEAD_DIM).astype(h.dtype) @ w_o


def shared_kv(x, c_act, kv_norm, kv_mod_w, kv_mod_b, w_k, w_v, cos, sin):
    B, S, _ = x.shape
    shift, scale = ada_mod(c_act, kv_mod_w, kv_mod_b, 2)
    hk = rms_norm(x, kv_norm) * (1.0 + scale) + shift
    k = apply_partial_rope((hk @ w_k).reshape(B, S, B_HEADS, 2, HEAD_DIM), cos, sin)
    v = (hk @ w_v).reshape(B, S, B_HEADS, 2 * HEAD_DIM)
    return k, v


def diff_attention(q, k, v, lam):
    B, S, H, _, Dh = q.shape
    nq = S // Q_BLOCK
    qb = q.reshape(B, nq, Q_BLOCK, H, 2, Dh).transpose(1, 0, 2, 3, 4, 5)
    kpos = jnp.arange(S)
    scale = Dh ** -0.5

    def one_block(args):
        qi, bi = args
        s = jnp.einsum('bqhcd,bkhcd->bhcqk', qi, k).astype(jnp.float32) * scale
        qpos = bi * Q_BLOCK + jnp.arange(Q_BLOCK)
        s = jnp.where((kpos[None, :] <= qpos[:, None])[None, None, None], s, NEG_INF)
        p = jax.nn.softmax(s, axis=-1)
        a = p[:, :, 0] - lam * p[:, :, 1]
        return jnp.einsum('bhqk,bkhe->bqhe', a.astype(v.dtype), v)

    o = lax.map(one_block, (qb, jnp.arange(nq)))
    return o.transpose(1, 0, 2, 3, 4).reshape(B, S, H, 2 * Dh)


def diff_mixer(h, k, v, w_q, lq1, lk1, lq2, lk2, subln, w_o, lam_init, cos, sin):
    B, S, _ = h.shape
    q = apply_partial_rope((h @ w_q).reshape(B, S, B_HEADS, 2, HEAD_DIM), cos, sin)
    f32 = jnp.float32
    lam = (jnp.exp(jnp.sum(lq1.astype(f32) * lk1.astype(f32)))
           - jnp.exp(jnp.sum(lq2.astype(f32) * lk2.astype(f32))) + lam_init)
    o = diff_attention(q, k, v, lam)
    o = rms_norm(o, subln, eps=SUBLN_EPS) * (1.0 - lam_init)
    return o.reshape(B, S, B_HEADS * 2 * HEAD_DIM) @ w_o


def conv_ffn(h, w_gate, w_up, conv_w, conv_b, w_down):
    S = h.shape[1]
    a = h @ w_gate
    ap = jnp.pad(a, ((0, 0), (CONV_WIDTH - 1, 0), (0, 0)))
    conv = conv_b
    for t in range(CONV_WIDTH):
        conv = conv + conv_w[t] * ap[:, t:t + S]
    return (jax.nn.gelu(conv, approximate=True) * (h @ w_up)) @ w_down


def setup_inputs(seed: int = 0) -> dict:
    key = jax.random.key(seed)
    ks = jax.random.split(key, 32)
    f32 = jnp.float32
    D, F = D_MODEL, D_FF
    G = len(A_GROUPS)
    a_inner = A_HEADS * HEAD_DIM
    b_inner = B_HEADS * 2 * HEAD_DIM

    def w(k, shape, fan_in, gain=1.0):
        return jax.random.normal(k, shape, f32) * (gain * fan_in ** -0.5)

    def gain(k, shape):
        return 1.0 + 0.05 * jax.random.normal(k, shape, f32)

    def bias(k, shape):
        return 0.01 * jax.random.normal(k, shape, f32)

    positions = (jnp.arange(SEQ, dtype=jnp.int32)[None, :]
                 + jax.random.randint(ks[2], (BATCH, 1), 0, 2048, dtype=jnp.int32))
    return {
        'x': jax.random.normal(ks[0], (BATCH, SEQ, D), f32),
        'c': jax.random.normal(ks[1], (BATCH, D), f32),
        'positions': positions,
        'mod_mix_w': w(ks[3], (DEPTH, D, 3 * D), D, 0.5),
        'mod_mix_b': bias(ks[4], (DEPTH, 3 * D)),
        'mod_ffn_w': w(ks[5], (DEPTH, D, 3 * D), D, 0.5),
        'mod_ffn_b': bias(ks[6], (DEPTH, 3 * D)),
        'norm_pre_mix': gain(ks[7], (DEPTH, D)),
        'norm_post_mix': gain(ks[8], (DEPTH, D)),
        'norm_pre_ffn': gain(ks[9], (DEPTH, D)),
        'norm_post_ffn': gain(ks[10], (DEPTH, D)),
        'ffn_w_gate': w(ks[11], (DEPTH, D, F), D),
        'ffn_w_up': w(ks[12], (DEPTH, D, F), D),
        'ffn_conv_w': w(ks[13], (DEPTH, CONV_WIDTH, F), CONV_WIDTH),
        'ffn_conv_b': bias(ks[14], (DEPTH, F)),
        'ffn_w_down': w(ks[15], (DEPTH, F, D), F),
        'a_w_qkv': w(ks[16], (N_A_LAYERS, D, G * 3 * a_inner), D),
        'a_w_o': w(ks[17], (N_A_LAYERS, a_inner, D), a_inner),
        'kv_norm': gain(ks[18], (D,)),
        'kv_mod_w': w(ks[19], (D, 2 * D), D, 0.5),
        'kv_mod_b': bias(ks[20], (2 * D,)),
        'b_w_k': w(ks[21], (D, b_inner), D),
        'b_w_v': w(ks[22], (D, b_inner), D),
        'b_w_q': w(ks[23], (N_B_LAYERS, D, b_inner), D),
        'b_lambda_q1': 0.1 * jax.random.normal(ks[24], (N_B_LAYERS, HEAD_DIM), f32),
        'b_lambda_k1': 0.1 * jax.random.normal(ks[25], (N_B_LAYERS, HEAD_DIM), f32),
        'b_lambda_q2': 0.1 * jax.random.normal(ks[26], (N_B_LAYERS, HEAD_DIM), f32),
        'b_lambda_k2': 0.1 * jax.random.normal(ks[27], (N_B_LAYERS, HEAD_DIM), f32),
        'b_subln': gain(ks[28], (N_B_LAYERS, 2 * HEAD_DIM)),
        'b_w_o': w(ks[29], (N_B_LAYERS, b_inner, D), b_inner),
    }


def reference(x, c, positions, mod_mix_w, mod_mix_b, mod_ffn_w, mod_ffn_b,
              norm_pre_mix, norm_post_mix, norm_pre_ffn, norm_post_ffn,
              ffn_w_gate, ffn_w_up, ffn_conv_w, ffn_conv_b, ffn_w_down,
              a_w_qkv, a_w_o, kv_norm, kv_mod_w, kv_mod_b, b_w_k, b_w_v,
              b_w_q, b_lambda_q1, b_lambda_k1, b_lambda_q2, b_lambda_k2, b_subln, b_w_o):
    c_act = jax.nn.silu(c)
    cos, sin = rope_tables(positions)
    k_sh, v_sh = None, None
    for l in range(DEPTH):
        shift, scale, gate = ada_mod(c_act, mod_mix_w[l], mod_mix_b[l], 3)
        if l == N_A_LAYERS:
            k_sh, v_sh = shared_kv(x, c_act, kv_norm, kv_mod_w, kv_mod_b, b_w_k, b_w_v, cos, sin)
        h = rms_norm(x, norm_pre_mix[l]) * (1.0 + scale) + shift
        if l < N_A_LAYERS:
            y = dilated_mixer(h, a_w_qkv[l], a_w_o[l], cos, sin)
        else:
            j = l - N_A_LAYERS
            lam_init = 0.8 - 0.6 * math.exp(-0.3 * l)
            y = diff_mixer(h, k_sh, v_sh, b_w_q[j], b_lambda_q1[j], b_lambda_k1[j],
                           b_lambda_q2[j], b_lambda_k2[j], b_subln[j], b_w_o[j], lam_init, cos, sin)
        x = x + gate * rms_norm(y, norm_post_mix[l])
        shift, scale, gate = ada_mod(c_act, mod_ffn_w[l], mod_ffn_b[l], 3)
        h = rms_norm(x, norm_pre_ffn[l]) * (1.0 + scale) + shift
        y = conv_ffn(h, ffn_w_gate[l], ffn_w_up[l], ffn_conv_w[l], ffn_conv_b[l], ffn_w_down[l])
        x = x + gate * rms_norm(y, norm_post_ffn[l])
    return x
```

```python
import functools
import math

import jax
import jax.numpy as jnp
from jax import lax
from jax.experimental import pallas as pl
from jax.experimental.pallas import tpu as pltpu

D_MODEL = 1024
HEAD_DIM = 64
ROT_DIM = HEAD_DIM // 4
ROT_HALF = ROT_DIM // 2
ROPE_THETA = 500000.0
A_GROUPS = ((128, 1), (512, 4), (2048, 16))
A_HEADS = 8
A_INNER = A_HEADS * HEAD_DIM
B_HEADS = D_MODEL // (2 * HEAD_DIM)
CONV_WIDTH = 3
NORM_EPS = 1e-6
SUBLN_EPS = 1e-5
NEG_INF = -1e30
QK_SCALE = HEAD_DIM ** -0.5

LANES = 128
SUBLANES = 8
WIN = 128
VMEM_LIMIT = 56 * 1024 * 1024

F32 = jnp.float32
BF16 = jnp.bfloat16


def _resident(shape):
    zeros = (0,) * len(shape)
    return pl.BlockSpec(shape, lambda *_: zeros, pipeline_mode=pl.Buffered(1))


def _params(n_axes, sequential=False):
    sem = ("arbitrary",) * n_axes if sequential else ("parallel",) * n_axes
    return pltpu.CompilerParams(dimension_semantics=sem, vmem_limit_bytes=VMEM_LIMIT)


def _rms(x, g, eps):
    ms = jnp.mean(x * x, axis=-1, keepdims=True)
    return x * lax.rsqrt(ms + eps) * g


def _rope128(y, c, sa, sb):
    return y * c + pltpu.roll(y, LANES - ROT_HALF, 1) * sa + pltpu.roll(y, ROT_HALF, 1) * sb


def _mod_kernel(c_ref, w_ref, b_ref, o_ref):
    c = c_ref[...]
    c_act = c / (1.0 + jnp.exp(-c))
    o_ref[...] = jnp.dot(c_act, w_ref[...], preferred_element_type=F32) + b_ref[...]


def _ada_mod(c, w, b, tn=1024):
    n_layers, d, n = w.shape
    bsz = c.shape[0]
    return pl.pallas_call(
        _mod_kernel,
        grid=(n_layers, n // tn),
        in_specs=[
            pl.BlockSpec((bsz, d), lambda l, j: (0, 0)),
            pl.BlockSpec((None, d, tn), lambda l, j: (l, 0, j)),
            pl.BlockSpec((None, 1, tn), lambda l, j: (l, 0, j)),
        ],
        out_specs=pl.BlockSpec((None, bsz, tn), lambda l, j: (l, 0, j)),
        out_shape=jax.ShapeDtypeStruct((n_layers, bsz, n), F32),
        compiler_params=_params(2),
        name="ada_mod",
    )(c, w, b.reshape(n_layers, 1, n))


def _rope_kernel(pos_ref, inv_ref, c_ref, sa_ref, sb_ref):
    ang = pos_ref[...].astype(F32) * inv_ref[...]
    cos = jnp.cos(ang)
    sin = jnp.sin(ang)
    lane = lax.broadcasted_iota(jnp.int32, ang.shape, 1) & (HEAD_DIM - 1)
    c_ref[...] = cos
    sa_ref[...] = jnp.where(lane < ROT_HALF, -sin, 0.0)
    sb_ref[...] = jnp.where((lane >= ROT_HALF) & (lane < ROT_DIM), sin, 0.0)


def _rope_tables(positions, ts=1024):
    bsz, seq = positions.shape
    inv = jnp.power(ROPE_THETA, -jnp.arange(0, ROT_DIM, 2, dtype=F32) / ROT_DIM)
    head = jnp.concatenate([inv, inv, jnp.zeros((HEAD_DIM - ROT_DIM,), F32)])
    inv_lane = jnp.tile(head, LANES // HEAD_DIM).reshape(1, LANES)
    spec = pl.BlockSpec((None, ts, LANES), lambda b, i: (b, i, 0))
    shape = jax.ShapeDtypeStruct((bsz, seq, LANES), F32)
    return pl.pallas_call(
        _rope_kernel,
        grid=(bsz, seq // ts),
        in_specs=[pl.BlockSpec((None, ts, 1), lambda b, i: (b, i, 0)),
                  pl.BlockSpec((1, LANES), lambda b, i: (0, 0))],
        out_specs=[spec, spec, spec],
        out_shape=[shape, shape, shape],
        compiler_params=_params(2),
        name="rope_tables",
    )(positions.reshape(bsz, seq, 1), inv_lane)


def _qkv0_kernel(x_ref, g_ref, sc_ref, sh_ref, w_ref, c_ref, sa_ref, sb_ref, o_ref):
    h = _rms(x_ref[...], g_ref[...], NORM_EPS) * (1.0 + sc_ref[...]) + sh_ref[...]
    hb = h.astype(BF16)
    c, sa, sb = c_ref[...], sa_ref[...], sb_ref[...]
    n_chunks = w_ref.shape[1] // A_INNER
    for ch in range(n_chunks):
        lo = ch * A_INNER
        y = jnp.dot(hb, w_ref[:, lo:lo + A_INNER], preferred_element_type=F32)
        kind = ch % 3
        if kind == 2:
            o_ref[:, lo:lo + A_INNER] = y.astype(BF16)
            continue
        for j in range(A_INNER // LANES):
            yj = _rope128(y[:, j * LANES:(j + 1) * LANES], c, sa, sb)
            if kind == 0:
                yj = yj * QK_SCALE
            o_ref[:, lo + j * LANES:lo + (j + 1) * LANES] = yj.astype(BF16)


def _qkv0(x, gain, scale, shift, w, tables, tm=512):
    bsz, seq, d = x.shape
    n = w.shape[1]
    row = pl.BlockSpec((None, 1, d), lambda b, i: (b, 0, 0))
    tab = pl.BlockSpec((None, tm, LANES), lambda b, i: (b, i, 0))
    return pl.pallas_call(
        _qkv0_kernel,
        grid=(bsz, seq // tm),
        in_specs=[pl.BlockSpec((None, tm, d), lambda b, i: (b, i, 0)),
                  _resident((1, d)), row, row, _resident((d, n)), tab, tab, tab],
        out_specs=pl.BlockSpec((None, tm, n), lambda b, i: (b, i, 0)),
        out_shape=jax.ShapeDtypeStruct((bsz, seq, n), BF16),
        compiler_params=_params(2),
        name="qkv0",
    )(x, gain.reshape(1, d), scale, shift, w, *tables)


def _dil_kernel(q_ref, kp_ref, k_ref, vp_ref, v_ref, o_ref, lse_ref, *, tq):
    first_tile = pl.program_id(2) == 0
    row = lax.broadcasted_iota(jnp.int32, (WIN, 2 * WIN), 0)
    col = lax.broadcasted_iota(jnp.int32, (WIN, 2 * WIN), 1)
    band = (col >= row) & (col <= row + WIN)
    band_first = band & ((col >= WIN) | jnp.logical_not(first_tile))
    lane = lax.broadcasted_iota(jnp.int32, (WIN, LANES), 1)
    for j in range(tq // WIN):
        rows = slice(j * WIN, (j + 1) * WIN)
        lse_acc = jnp.zeros((WIN, LANES), F32)
        for h in range(A_HEADS):
            cols = slice(h * HEAD_DIM, (h + 1) * HEAD_DIM)
            q = q_ref[rows, cols]
            if j == 0:
                kk = jnp.concatenate([kp_ref[:, cols], k_ref[0:WIN, cols]], axis=0)
                vv = jnp.concatenate([vp_ref[:, cols], v_ref[0:WIN, cols]], axis=0)
            else:
                kk = k_ref[(j - 1) * WIN:(j + 1) * WIN, cols]
                vv = v_ref[(j - 1) * WIN:(j + 1) * WIN, cols]
            s = lax.dot_general(q, kk, (((1,), (1,)), ((), ())), preferred_element_type=F32)
            s = jnp.where(band_first if j == 0 else band, s, NEG_INF)
            mx = jnp.max(s, axis=-1, keepdims=True)
            p = jnp.exp(s - mx)
            den = jnp.sum(p, axis=-1, keepdims=True)
            o = jnp.dot(p.astype(BF16), vv, preferred_element_type=F32)
            o_ref[rows, cols] = o / den
            lse_acc = jnp.where(lane == h, mx + jnp.log(den), lse_acc)
        lse_ref[rows, :] = lse_acc


def _dilated_attention(qkv, group, dilation, tq=256):
    bsz, seq, n = qkv.shape
    length = seq // dilation
    per_row = n // A_INNER
    view = qkv.reshape(bsz, length, dilation * n)
    sub = tq // WIN

    def cur(kind):
        return pl.BlockSpec((None, tq, A_INNER),
                            lambda b, r, i: (b, i, r * per_row + 3 * group + kind))

    def prev(kind):
        return pl.BlockSpec((None, WIN, A_INNER),
                            lambda b, r, i: (b, jnp.maximum(i * sub - 1, 0), r * per_row + 3 * group + kind))

    o, lse = pl.pallas_call(
        functools.partial(_dil_kernel, tq=tq),
        grid=(bsz, dilation, length // tq),
        in_specs=[cur(0), prev(1), cur(1), prev(2), cur(2)],
        out_specs=[pl.BlockSpec((None, tq, A_INNER), lambda b, r, i: (b, i, r)),
                   pl.BlockSpec((None, tq, LANES), lambda b, r, i: (b, i, r))],
        out_shape=[jax.ShapeDtypeStruct((bsz, length, dilation * A_INNER), F32),
                   jax.ShapeDtypeStruct((bsz, length, dilation * LANES), F32)],
        compiler_params=_params(3),
        name=f"dilated_attn_d{dilation}",
    )(view, view, view, view, view)
    return o.reshape(bsz, seq, A_INNER), lse.reshape(bsz, seq, LANES)


def _out0_kernel(o0_ref, o1_ref, o2_ref, l0_ref, l1_ref, l2_ref, x_ref, w_ref, g_ref, gate_ref, out_ref):
    l0, l1, l2 = l0_ref[...], l1_ref[...], l2_ref[...]
    m = jnp.maximum(jnp.maximum(l0, l1), l2)
    e0, e1, e2 = jnp.exp(l0 - m), jnp.exp(l1 - m), jnp.exp(l2 - m)
    inv = 1.0 / (e0 + e1 + e2)
    alphas = (e0 * inv, e1 * inv, e2 * inv)
    o_refs = (o0_ref, o1_ref, o2_ref)
    lane = lax.broadcasted_iota(jnp.int32, l0.shape, 1)
    y = None
    for hp in range(A_HEADS // 2):
        cols = slice(hp * LANES, (hp + 1) * LANES)
        mixed = None
        for a, o_ref in zip(alphas, o_refs):
            wgt = jnp.where(lane < HEAD_DIM, a[:, 2 * hp:2 * hp + 1], a[:, 2 * hp + 1:2 * hp + 2])
            term = wgt * o_ref[:, cols]
            mixed = term if mixed is None else mixed + term
        part = jnp.dot(mixed.astype(BF16), w_ref[cols, :], preferred_element_type=F32)
        y = part if y is None else y + part
    out_ref[...] = x_ref[...] + gate_ref[...] * _rms(y, g_ref[...], NORM_EPS)


def _out0(outs, lses, x, w, gain, gate, tm=512):
    bsz, seq, d = x.shape
    o_spec = pl.BlockSpec((None, tm, A_INNER), lambda b, i: (b, i, 0))
    l_spec = pl.BlockSpec((None, tm, LANES), lambda b, i: (b, i, 0))
    x_spec = pl.BlockSpec((None, tm, d), lambda b, i: (b, i, 0))
    return pl.pallas_call(
        _out0_kernel,
        grid=(bsz, seq // tm),
        in_specs=[o_spec] * 3 + [l_spec] * 3 + [
            x_spec, _resident((A_INNER, d)), _resident((1, d)),
            pl.BlockSpec((None, 1, d), lambda b, i: (b, 0, 0))],
        out_specs=x_spec,
        out_shape=jax.ShapeDtypeStruct((bsz, seq, d), F32),
        compiler_params=_params(2),
        name="mix_out0",
    )(*outs, *lses, x, w, gain.reshape(1, d), gate)


def _ffn_kernel(x_ref, gpre_ref, sc_ref, sh_ref, gate_ref, wg_ref, wu_ref, cw_ref, cb_ref, wd_ref,
                gpost_ref, out_ref, a_buf, *, tm, fc):
    @pl.when(pl.program_id(1) == 0)
    def _():
        a_buf[0:SUBLANES, :] = jnp.zeros((SUBLANES, a_buf.shape[1]), F32)

    x = x_ref[...]
    h = _rms(x, gpre_ref[...], NORM_EPS) * (1.0 + sc_ref[...]) + sh_ref[...]
    hb = h.astype(BF16)
    d_ff = wg_ref.shape[1]
    y = None
    for lo in range(0, d_ff, fc):
        cols = slice(lo, lo + fc)
        a_buf[SUBLANES:SUBLANES + tm, cols] = jnp.dot(hb, wg_ref[:, cols], preferred_element_type=F32)
        u = jnp.dot(hb, wu_ref[:, cols], preferred_element_type=F32)
        conv = cb_ref[:, cols]
        for t in range(CONV_WIDTH):
            start = SUBLANES - (CONV_WIDTH - 1) + t
            conv = conv + cw_ref[t:t + 1, cols] * a_buf[start:start + tm, cols]
        inner = math.sqrt(2.0 / math.pi) * (conv + 0.044715 * (conv * conv * conv))
        act = 0.5 * conv * (1.0 + jnp.tanh(inner)) * u
        part = jnp.dot(act.astype(BF16), wd_ref[cols, :], preferred_element_type=F32)
        y = part if y is None else y + part
        a_buf[0:SUBLANES, cols] = a_buf[tm:tm + SUBLANES, cols]
    out_ref[...] = x + gate_ref[...] * _rms(y, gpost_ref[...], NORM_EPS)


def _ffn(x, gpre, scale, shift, gate, wg, wu, cw, cb, wd, gpost, tm=512, fc=256):
    bsz, seq, d = x.shape
    d_ff = wg.shape[1]
    x_spec = pl.BlockSpec((None, tm, d), lambda b, i: (b, i, 0))
    row = pl.BlockSpec((None, 1, d), lambda b, i: (b, 0, 0))
    return pl.pallas_call(
        functools.partial(_ffn_kernel, tm=tm, fc=fc),
        grid=(bsz, seq // tm),
        in_specs=[x_spec, _resident((1, d)), row, row, row,
                  _resident((d, d_ff)), _resident((d, d_ff)), _resident((CONV_WIDTH, d_ff)),
                  _resident((1, d_ff)), _resident((d_ff, d)), _resident((1, d))],
        out_specs=x_spec,
        out_shape=jax.ShapeDtypeStruct((bsz, seq, d), F32),
        scratch_shapes=[pltpu.VMEM((tm + SUBLANES, d_ff), F32)],
        compiler_params=_params(2, sequential=True),
        name="conv_ffn",
    )(x, gpre.reshape(1, d), scale, shift, gate, wg, wu, cw, cb.reshape(1, d_ff), wd, gpost.reshape(1, d))


def _qkv1_kernel(x_ref, gq_ref, scq_ref, shq_ref, gkv_ref, sckv_ref, shkv_ref, wq_ref, wk_ref, wv_ref,
                 c_ref, sa_ref, sb_ref, q_ref, k_ref, v_ref):
    x = x_ref[...]
    rstd = lax.rsqrt(jnp.mean(x * x, axis=-1, keepdims=True) + NORM_EPS)
    xn = x * rstd
    hq = ((xn * gq_ref[...]) * (1.0 + scq_ref[...]) + shq_ref[...]).astype(BF16)
    hkv = ((xn * gkv_ref[...]) * (1.0 + sckv_ref[...]) + shkv_ref[...]).astype(BF16)
    c, sa, sb = c_ref[...], sa_ref[...], sb_ref[...]
    d = x.shape[1]
    for j in range(d // LANES):
        cols = slice(j * LANES, (j + 1) * LANES)
        yq = jnp.dot(hq, wq_ref[:, cols], preferred_element_type=F32)
        q_ref[:, cols] = (_rope128(yq, c, sa, sb) * QK_SCALE).astype(BF16)
        yk = jnp.dot(hkv, wk_ref[:, cols], preferred_element_type=F32)
        k_ref[:, cols] = _rope128(yk, c, sa, sb).astype(BF16)
    v_ref[...] = jnp.dot(hkv, wv_ref[...], preferred_element_type=F32).astype(BF16)


def _qkv1(x, gq, scq, shq, gkv, sckv, shkv, wq, wk, wv, tables, tm=512):
    bsz, seq, d = x.shape
    x_spec = pl.BlockSpec((None, tm, d), lambda b, i: (b, i, 0))
    row = pl.BlockSpec((None, 1, d), lambda b, i: (b, 0, 0))
    tab = pl.BlockSpec((None, tm, LANES), lambda b, i: (b, i, 0))
    shape = jax.ShapeDtypeStruct((bsz, seq, d), BF16)
    return pl.pallas_call(
        _qkv1_kernel,
        grid=(bsz, seq // tm),
        in_specs=[x_spec, _resident((1, d)), row, row, _resident((1, d)), row, row,
                  _resident((d, d)), _resident((d, d)), _resident((d, d)), tab, tab, tab],
        out_specs=[x_spec, x_spec, x_spec],
        out_shape=[shape, shape, shape],
        compiler_params=_params(2),
        name="qkv1",
    )(x, gq.reshape(1, d), scq, shq, gkv.reshape(1, d), sckv, shkv, wq, wk, wv, *tables)


def _diff_kernel(lq1_ref, lk1_ref, lq2_ref, lk2_ref, g_ref, q_ref, k_ref, v_ref, o_ref,
                 m_ref, l_ref, acc_ref, *, tq, lam_init):
    i = pl.program_id(2)
    lam = (jnp.exp(jnp.sum(lq1_ref[...] * lk1_ref[...], axis=-1, keepdims=True))
           - jnp.exp(jnp.sum(lq2_ref[...] * lk2_ref[...], axis=-1, keepdims=True)) + lam_init)
    m_ref[...] = jnp.full(m_ref.shape, NEG_INF, F32)
    l_ref[...] = jnp.zeros(l_ref.shape, F32)
    acc_ref[...] = jnp.zeros(acc_ref.shape, F32)
    q = q_ref[...]
    row = lax.broadcasted_iota(jnp.int32, (tq, tq), 0)
    col = lax.broadcasted_iota(jnp.int32, (tq, tq), 1)
    causal = col <= row

    def block(jb, masked):
        start = pl.multiple_of(jb * tq, tq)
        k = k_ref[pl.ds(start, tq), :]
        v = v_ref[pl.ds(start, tq), :]
        for c in range(2):
            cols = slice(c * HEAD_DIM, (c + 1) * HEAD_DIM)
            s = lax.dot_general(q[:, cols], k[:, cols], (((1,), (1,)), ((), ())),
                                preferred_element_type=F32)
            if masked:
                s = jnp.where(causal, s, NEG_INF)
            m_old = m_ref[c]
            m_new = jnp.maximum(m_old, jnp.max(s, axis=-1, keepdims=True))
            alpha = jnp.exp(m_old - m_new)
            p = jnp.exp(s - m_new)
            l_ref[c] = alpha * l_ref[c] + jnp.sum(p, axis=-1, keepdims=True)
            acc_ref[c] = alpha * acc_ref[c] + jnp.dot(p.astype(BF16), v, preferred_element_type=F32)
            m_ref[c] = m_new

    def body(jb, carry):
        block(jb, False)
        return carry

    lax.fori_loop(0, i, body, 0)
    block(i, True)
    o = acc_ref[0] / l_ref[0] - lam * (acc_ref[1] / l_ref[1])
    o_ref[...] = (_rms(o, g_ref[...], SUBLN_EPS) * (1.0 - lam_init)).astype(BF16)


def _diff_attention(q, k, v, lq1, lk1, lq2, lk2, subln, lam_init, tq=512):
    bsz, seq, d = q.shape
    width = 2 * HEAD_DIM
    vec = _resident((1, HEAD_DIM))
    kv_spec = pl.BlockSpec((None, seq, width), lambda b, h, i: (b, 0, h))
    qo_spec = pl.BlockSpec((None, tq, width), lambda b, h, i: (b, i, h))
    return pl.pallas_call(
        functools.partial(_diff_kernel, tq=tq, lam_init=lam_init),
        grid=(bsz, B_HEADS, seq // tq),
        in_specs=[vec, vec, vec, vec, _resident((1, width)), qo_spec, kv_spec, kv_spec],
        out_specs=qo_spec,
        out_shape=jax.ShapeDtypeStruct((bsz, seq, d), BF16),
        scratch_shapes=[pltpu.VMEM((2, tq, 1), F32), pltpu.VMEM((2, tq, 1), F32),
                        pltpu.VMEM((2, tq, width), F32)],
        compiler_params=_params(3),
        name="diff_attn",
    )(lq1.reshape(1, HEAD_DIM), lk1.reshape(1, HEAD_DIM), lq2.reshape(1, HEAD_DIM),
      lk2.reshape(1, HEAD_DIM), subln.reshape(1, width), q, k, v)


def _out1_kernel(o_ref, x_ref, w_ref, g_ref, gate_ref, out_ref):
    y = jnp.dot(o_ref[...], w_ref[...], preferred_element_type=F32)
    out_ref[...] = x_ref[...] + gate_ref[...] * _rms(y, g_ref[...], NORM_EPS)


def _out1(o, x, w, gain, gate, tm=512):
    bsz, seq, d = x.shape
    spec = pl.BlockSpec((None, tm, d), lambda b, i: (b, i, 0))
    return pl.pallas_call(
        _out1_kernel,
        grid=(bsz, seq // tm),
        in_specs=[spec, spec, _resident((d, d)), _resident((1, d)),
                  pl.BlockSpec((None, 1, d), lambda b, i: (b, 0, 0))],
        out_specs=spec,
        out_shape=jax.ShapeDtypeStruct((bsz, seq, d), F32),
        compiler_params=_params(2),
        name="proj_out1",
    )(o, x, w, gain.reshape(1, d), gate)


def _split3(m):
    return jnp.split(m[:, None, :], 3, axis=-1)


def kernel(x, c, positions, mod_mix_w, mod_mix_b, mod_ffn_w, mod_ffn_b, norm_pre_mix, norm_post_mix, norm_pre_ffn, norm_post_ffn, ffn_w_gate, ffn_w_up, ffn_conv_w, ffn_conv_b, ffn_w_down, a_w_qkv, a_w_o, kv_norm, kv_mod_w, kv_mod_b, b_w_k, b_w_v, b_w_q, b_lambda_q1, b_lambda_k1, b_lambda_q2, b_lambda_k2, b_subln, b_w_o):
    mix_mod = _ada_mod(c, mod_mix_w, mod_mix_b)
    ffn_mod = _ada_mod(c, mod_ffn_w, mod_ffn_b)
    kv_mod = _ada_mod(c, kv_mod_w[None], kv_mod_b[None])[0]
    tables = _rope_tables(positions)

    def ffn(x, l):
        shift, scale, gate = _split3(ffn_mod[l])
        return _ffn(x, norm_pre_ffn[l], scale, shift, gate, ffn_w_gate[l].astype(BF16),
                    ffn_w_up[l].astype(BF16), ffn_conv_w[l], ffn_conv_b[l],
                    ffn_w_down[l].astype(BF16), norm_post_ffn[l])

    shift, scale, gate = _split3(mix_mod[0])
    qkv = _qkv0(x, norm_pre_mix[0], scale, shift, a_w_qkv[0].astype(BF16), tables)
    outs, lses = [], []
    for g, (_, dilation) in enumerate(A_GROUPS):
        o, lse = _dilated_attention(qkv, g, dilation)
        outs.append(o)
        lses.append(lse)
    x = _out0(outs, lses, x, a_w_o[0].astype(BF16), norm_post_mix[0], gate)
    x = ffn(x, 0)

    shift, scale, gate = _split3(mix_mod[1])
    kv_shift, kv_scale = jnp.split(kv_mod[:, None, :], 2, axis=-1)
    q, k, v = _qkv1(x, norm_pre_mix[1], scale, shift, kv_norm, kv_scale, kv_shift,
                    b_w_q[0].astype(BF16), b_w_k.astype(BF16), b_w_v.astype(BF16), tables)
    lam_init = 0.8 - 0.6 * math.exp(-0.3 * 1)
    o = _diff_attention(q, k, v, b_lambda_q1[0], b_lambda_k1[0], b_lambda_q2[0], b_lambda_k2[0],
                        b_subln[0], lam_init)
    x = _out1(o, x, b_w_o[0].astype(BF16), norm_post_mix[1], gate)
    x = ffn(x, 1)
    return x
```

```python
import functools
import math

import jax
import jax.numpy as jnp
from jax import lax
from jax.experimental import pallas as pl
from jax.experimental.pallas import tpu as pltpu

D_MODEL = 1024
HEAD_DIM = 64
ROT_DIM = HEAD_DIM // 4
ROT_HALF = ROT_DIM // 2
ROPE_THETA = 500000.0
A_GROUPS = ((128, 1), (512, 4), (2048, 16))
A_HEADS = 8
A_INNER = A_HEADS * HEAD_DIM
B_HEADS = D_MODEL // (2 * HEAD_DIM)
CONV_WIDTH = 3
NORM_EPS = 1e-6
SUBLN_EPS = 1e-5
NEG_INF = -1e30
QK_SCALE = HEAD_DIM ** -0.5

LANES = 128
SUBLANES = 8
WIN = 128
PROJ_CHUNK = 512
VT_CHUNK = 256
DIFF_TILE = 512
VMEM_LIMIT = 56 * 1024 * 1024

F32 = jnp.float32
BF16 = jnp.bfloat16


def _resident(shape):
    zeros = (0,) * len(shape)
    return pl.BlockSpec(shape, lambda *_: zeros, pipeline_mode=pl.Buffered(1))


def _params(n_axes, sequential=False):
    sem = ("arbitrary",) * n_axes if sequential else ("parallel",) * n_axes
    return pltpu.CompilerParams(dimension_semantics=sem, vmem_limit_bytes=VMEM_LIMIT)


def _rms(x, g, eps):
    ms = jnp.mean(x * x, axis=-1, keepdims=True)
    return x * lax.rsqrt(ms + eps) * g


def _rope128(y, c, sa, sb):
    return y * c + pltpu.roll(y, LANES - ROT_HALF, 1) * sa + pltpu.roll(y, ROT_HALF, 1) * sb


def _mod_kernel(c_ref, w_ref, b_ref, o_ref):
    c = c_ref[...]
    c_act = c / (1.0 + jnp.exp(-c))
    o_ref[...] = jnp.dot(c_act, w_ref[...], preferred_element_type=F32) + b_ref[...]


def _ada_mod(c, w, b, tn=1024):
    n_layers, d, n = w.shape
    bsz = c.shape[0]
    return pl.pallas_call(
        _mod_kernel,
        grid=(n_layers, n // tn),
        in_specs=[
            pl.BlockSpec((bsz, d), lambda l, j: (0, 0)),
            pl.BlockSpec((None, d, tn), lambda l, j: (l, 0, j)),
            pl.BlockSpec((None, 1, tn), lambda l, j: (l, 0, j)),
        ],
        out_specs=pl.BlockSpec((None, bsz, tn), lambda l, j: (l, 0, j)),
        out_shape=jax.ShapeDtypeStruct((n_layers, bsz, n), F32),
        compiler_params=_params(2),
        name="ada_mod",
    )(c, w, b.reshape(n_layers, 1, n))


def _rope_kernel(pos_ref, inv_ref, c_ref, sa_ref, sb_ref):
    ang = pos_ref[...].astype(F32) * inv_ref[...]
    cos = jnp.cos(ang)
    sin = jnp.sin(ang)
    lane = lax.broadcasted_iota(jnp.int32, ang.shape, 1) & (HEAD_DIM - 1)
    c_ref[...] = cos
    sa_ref[...] = jnp.where(lane < ROT_HALF, -sin, 0.0)
    sb_ref[...] = jnp.where((lane >= ROT_HALF) & (lane < ROT_DIM), sin, 0.0)


def _rope_tables(positions, ts=1024):
    bsz, seq = positions.shape
    inv = jnp.power(ROPE_THETA, -jnp.arange(0, ROT_DIM, 2, dtype=F32) / ROT_DIM)
    head = jnp.concatenate([inv, inv, jnp.zeros((HEAD_DIM - ROT_DIM,), F32)])
    inv_lane = jnp.tile(head, LANES // HEAD_DIM).reshape(1, LANES)
    spec = pl.BlockSpec((None, ts, LANES), lambda b, i: (b, i, 0))
    shape = jax.ShapeDtypeStruct((bsz, seq, LANES), F32)
    return pl.pallas_call(
        _rope_kernel,
        grid=(bsz, seq // ts),
        in_specs=[pl.BlockSpec((None, ts, 1), lambda b, i: (b, i, 0)),
                  pl.BlockSpec((1, LANES), lambda b, i: (0, 0))],
        out_specs=[spec, spec, spec],
        out_shape=[shape, shape, shape],
        compiler_params=_params(2),
        name="rope_tables",
    )(positions.reshape(bsz, seq, 1), inv_lane)


def _qkv0_kernel(x_ref, g_ref, sc_ref, sh_ref, w_ref, c_ref, sa_ref, sb_ref, o_ref):
    h = _rms(x_ref[...], g_ref[...], NORM_EPS) * (1.0 + sc_ref[...]) + sh_ref[...]
    hb = h.astype(BF16)
    c, sa, sb = c_ref[...], sa_ref[...], sb_ref[...]
    n_chunks = w_ref.shape[1] // A_INNER
    for ch in range(n_chunks):
        lo = ch * A_INNER
        y = jnp.dot(hb, w_ref[:, lo:lo + A_INNER], preferred_element_type=F32)
        kind = ch % 3
        if kind == 2:
            o_ref[:, lo:lo + A_INNER] = y.astype(BF16)
            continue
        for j in range(A_INNER // LANES):
            yj = _rope128(y[:, j * LANES:(j + 1) * LANES], c, sa, sb)
            if kind == 0:
                yj = yj * QK_SCALE
            o_ref[:, lo + j * LANES:lo + (j + 1) * LANES] = yj.astype(BF16)


def _qkv0(x, gain, scale, shift, w, tables, tm=512):
    bsz, seq, d = x.shape
    n = w.shape[1]
    row = pl.BlockSpec((None, 1, d), lambda b, i: (b, 0, 0))
    tab = pl.BlockSpec((None, tm, LANES), lambda b, i: (b, i, 0))
    return pl.pallas_call(
        _qkv0_kernel,
        grid=(bsz, seq // tm),
        in_specs=[pl.BlockSpec((None, tm, d), lambda b, i: (b, i, 0)),
                  _resident((1, d)), row, row, _resident((d, n)), tab, tab, tab],
        out_specs=pl.BlockSpec((None, tm, n), lambda b, i: (b, i, 0)),
        out_shape=jax.ShapeDtypeStruct((bsz, seq, n), BF16),
        compiler_params=_params(2),
        name="qkv0",
    )(x, gain.reshape(1, d), scale, shift, w, *tables)


def _dil_kernel(q_ref, kp_ref, k_ref, vp_ref, v_ref, o_ref, lse_ref, *, tq):
    first_tile = pl.program_id(2) == 0
    row = lax.broadcasted_iota(jnp.int32, (WIN, 2 * WIN), 0)
    col = lax.broadcasted_iota(jnp.int32, (WIN, 2 * WIN), 1)
    band = (col >= row) & (col <= row + WIN)
    band_first = band & ((col >= WIN) | jnp.logical_not(first_tile))
    lane = lax.broadcasted_iota(jnp.int32, (WIN, LANES), 1)
    for j in range(tq // WIN):
        rows = slice(j * WIN, (j + 1) * WIN)
        lse_acc = jnp.zeros((WIN, LANES), F32)
        for h in range(A_HEADS):
            cols = slice(h * HEAD_DIM, (h + 1) * HEAD_DIM)
            q = q_ref[rows, cols]
            if j == 0:
                kk = jnp.concatenate([kp_ref[:, cols], k_ref[0:WIN, cols]], axis=0)
                vv = jnp.concatenate([vp_ref[:, cols], v_ref[0:WIN, cols]], axis=0)
            else:
                kk = k_ref[(j - 1) * WIN:(j + 1) * WIN, cols]
                vv = v_ref[(j - 1) * WIN:(j + 1) * WIN, cols]
            s = lax.dot_general(q, kk, (((1,), (1,)), ((), ())), preferred_element_type=F32)
            s = jnp.where(band_first if j == 0 else band, s, NEG_INF)
            mx = jnp.max(s, axis=-1, keepdims=True)
            p = jnp.exp(s - mx)
            den = jnp.sum(p, axis=-1, keepdims=True)
            o = jnp.dot(p.astype(BF16), vv, preferred_element_type=F32)
            o_ref[rows, cols] = o / den
            lse_acc = jnp.where(lane == h, mx + jnp.log(den), lse_acc)
        lse_ref[rows, :] = lse_acc


def _dilated_attention(qkv, group, dilation, tq=256):
    bsz, seq, n = qkv.shape
    length = seq // dilation
    per_row = n // A_INNER
    view = qkv.reshape(bsz, length, dilation * n)
    sub = tq // WIN

    def cur(kind):
        return pl.BlockSpec((None, tq, A_INNER),
                            lambda b, r, i: (b, i, r * per_row + 3 * group + kind))

    def prev(kind):
        return pl.BlockSpec((None, WIN, A_INNER),
                            lambda b, r, i: (b, jnp.maximum(i * sub - 1, 0), r * per_row + 3 * group + kind))

    o, lse = pl.pallas_call(
        functools.partial(_dil_kernel, tq=tq),
        grid=(bsz, dilation, length // tq),
        in_specs=[cur(0), prev(1), cur(1), prev(2), cur(2)],
        out_specs=[pl.BlockSpec((None, tq, A_INNER), lambda b, r, i: (b, i, r)),
                   pl.BlockSpec((None, tq, LANES), lambda b, r, i: (b, i, r))],
        out_shape=[jax.ShapeDtypeStruct((bsz, length, dilation * A_INNER), F32),
                   jax.ShapeDtypeStruct((bsz, length, dilation * LANES), F32)],
        compiler_params=_params(3),
        name=f"dilated_attn_d{dilation}",
    )(view, view, view, view, view)
    return o.reshape(bsz, seq, A_INNER), lse.reshape(bsz, seq, LANES)


def _out0_kernel(o0_ref, o1_ref, o2_ref, l0_ref, l1_ref, l2_ref, x_ref, w_ref, g_ref, gate_ref, out_ref):
    l0, l1, l2 = l0_ref[...], l1_ref[...], l2_ref[...]
    m = jnp.maximum(jnp.maximum(l0, l1), l2)
    e0, e1, e2 = jnp.exp(l0 - m), jnp.exp(l1 - m), jnp.exp(l2 - m)
    inv = 1.0 / (e0 + e1 + e2)
    alphas = (e0 * inv, e1 * inv, e2 * inv)
    o_refs = (o0_ref, o1_ref, o2_ref)
    lane = lax.broadcasted_iota(jnp.int32, l0.shape, 1)
    y = None
    for hp in range(A_HEADS // 2):
        cols = slice(hp * LANES, (hp + 1) * LANES)
        mixed = None
        for a, o_ref in zip(alphas, o_refs):
            wgt = jnp.where(lane < HEAD_DIM, a[:, 2 * hp:2 * hp + 1], a[:, 2 * hp + 1:2 * hp + 2])
            term = wgt * o_ref[:, cols]
            mixed = term if mixed is None else mixed + term
        part = jnp.dot(mixed.astype(BF16), w_ref[cols, :], preferred_element_type=F32)
        y = part if y is None else y + part
    out_ref[...] = x_ref[...] + gate_ref[...] * _rms(y, g_ref[...], NORM_EPS)


def _out0(outs, lses, x, w, gain, gate, tm=512):
    bsz, seq, d = x.shape
    o_spec = pl.BlockSpec((None, tm, A_INNER), lambda b, i: (b, i, 0))
    l_spec = pl.BlockSpec((None, tm, LANES), lambda b, i: (b, i, 0))
    x_spec = pl.BlockSpec((None, tm, d), lambda b, i: (b, i, 0))
    return pl.pallas_call(
        _out0_kernel,
        grid=(bsz, seq // tm),
        in_specs=[o_spec] * 3 + [l_spec] * 3 + [
            x_spec, _resident((A_INNER, d)), _resident((1, d)),
            pl.BlockSpec((None, 1, d), lambda b, i: (b, 0, 0))],
        out_specs=x_spec,
        out_shape=jax.ShapeDtypeStruct((bsz, seq, d), F32),
        compiler_params=_params(2),
        name="mix_out0",
    )(*outs, *lses, x, w, gain.reshape(1, d), gate)


def _ffn_kernel(x_ref, gpre_ref, sc_ref, sh_ref, gate_ref, wg_ref, wu_ref, cw_ref, cb_ref, wd_ref,
                gpost_ref, out_ref, a_buf, *, tm, fc):
    @pl.when(pl.program_id(1) == 0)
    def _():
        a_buf[0:SUBLANES, :] = jnp.zeros((SUBLANES, a_buf.shape[1]), F32)

    x = x_ref[...]
    h = _rms(x, gpre_ref[...], NORM_EPS) * (1.0 + sc_ref[...]) + sh_ref[...]
    hb = h.astype(BF16)
    d_ff = wg_ref.shape[1]
    y = None
    for lo in range(0, d_ff, fc):
        cols = slice(lo, lo + fc)
        a_buf[SUBLANES:SUBLANES + tm, cols] = jnp.dot(hb, wg_ref[:, cols], preferred_element_type=F32)
        u = jnp.dot(hb, wu_ref[:, cols], preferred_element_type=F32)
        conv = cb_ref[:, cols]
        for t in range(CONV_WIDTH):
            start = SUBLANES - (CONV_WIDTH - 1) + t
            conv = conv + cw_ref[t:t + 1, cols] * a_buf[start:start + tm, cols]
        inner = math.sqrt(2.0 / math.pi) * (conv + 0.044715 * (conv * conv * conv))
        act = 0.5 * conv * (1.0 + jnp.tanh(inner)) * u
        part = jnp.dot(act.astype(BF16), wd_ref[cols, :], preferred_element_type=F32)
        y = part if y is None else y + part
        a_buf[0:SUBLANES, cols] = a_buf[tm:tm + SUBLANES, cols]
    out_ref[...] = x + gate_ref[...] * _rms(y, gpost_ref[...], NORM_EPS)


def _ffn(x, gpre, scale, shift, gate, wg, wu, cw, cb, wd, gpost, tm=512, fc=256):
    bsz, seq, d = x.shape
    d_ff = wg.shape[1]
    x_spec = pl.BlockSpec((None, tm, d), lambda b, i: (b, i, 0))
    row = pl.BlockSpec((None, 1, d), lambda b, i: (b, 0, 0))
    return pl.pallas_call(
        functools.partial(_ffn_kernel, tm=tm, fc=fc),
        grid=(bsz, seq // tm),
        in_specs=[x_spec, _resident((1, d)), row, row, row,
                  _resident((d, d_ff)), _resident((d, d_ff)), _resident((CONV_WIDTH, d_ff)),
                  _resident((1, d_ff)), _resident((d_ff, d)), _resident((1, d))],
        out_specs=x_spec,
        out_shape=jax.ShapeDtypeStruct((bsz, seq, d), F32),
        scratch_shapes=[pltpu.VMEM((tm + SUBLANES, d_ff), F32)],
        compiler_params=_params(2, sequential=True),
        name="conv_ffn",
    )(x, gpre.reshape(1, d), scale, shift, gate, wg, wu, cw, cb.reshape(1, d_ff), wd, gpost.reshape(1, d))


def _qkv1_kernel(x_ref, gq_ref, scq_ref, shq_ref, gkv_ref, sckv_ref, shkv_ref, wq_ref, wk_ref, wvt_ref,
                 c_ref, sa_ref, sb_ref, q_ref, k_ref, vt_ref):
    x = x_ref[...]
    rstd = lax.rsqrt(jnp.mean(x * x, axis=-1, keepdims=True) + NORM_EPS)
    xn = x * rstd
    hq = ((xn * gq_ref[...]) * (1.0 + scq_ref[...]) + shq_ref[...]).astype(BF16)
    hkv = ((xn * gkv_ref[...]) * (1.0 + sckv_ref[...]) + shkv_ref[...]).astype(BF16)
    c, sa, sb = c_ref[...], sa_ref[...], sb_ref[...]
    d = x.shape[1]
    for lo in range(0, d, PROJ_CHUNK):
        yq = jnp.dot(hq, wq_ref[:, lo:lo + PROJ_CHUNK], preferred_element_type=F32)
        yk = jnp.dot(hkv, wk_ref[:, lo:lo + PROJ_CHUNK], preferred_element_type=F32)
        for j in range(PROJ_CHUNK // LANES):
            cols = slice(lo + j * LANES, lo + (j + 1) * LANES)
            sub = slice(j * LANES, (j + 1) * LANES)
            q_ref[:, cols] = (_rope128(yq[:, sub], c, sa, sb) * QK_SCALE).astype(BF16)
            k_ref[:, cols] = _rope128(yk[:, sub], c, sa, sb).astype(BF16)
    for lo in range(0, d, VT_CHUNK):
        vt = lax.dot_general(wvt_ref[lo:lo + VT_CHUNK, :], hkv, (((1,), (1,)), ((), ())),
                             preferred_element_type=F32)
        vt_ref[lo:lo + VT_CHUNK, :] = vt.astype(BF16)


def _qkv1(x, gq, scq, shq, gkv, sckv, shkv, wq, wk, wvt, tables, tm):
    bsz, seq, d = x.shape
    x_spec = pl.BlockSpec((None, tm, d), lambda b, i: (b, i, 0))
    row = pl.BlockSpec((None, 1, d), lambda b, i: (b, 0, 0))
    tab = pl.BlockSpec((None, tm, LANES), lambda b, i: (b, i, 0))
    shape = jax.ShapeDtypeStruct((bsz, seq, d), BF16)
    return pl.pallas_call(
        _qkv1_kernel,
        grid=(bsz, seq // tm),
        in_specs=[x_spec, _resident((1, d)), row, row, _resident((1, d)), row, row,
                  _resident((d, d)), _resident((d, d)), _resident((d, d)), tab, tab, tab],
        out_specs=[x_spec, x_spec, pl.BlockSpec((None, None, d, tm), lambda b, i: (b, i, 0, 0))],
        out_shape=[shape, shape, jax.ShapeDtypeStruct((bsz, seq // tm, d, tm), BF16)],
        compiler_params=_params(2),
        name="qkv1",
    )(x, gq.reshape(1, d), scq, shq, gkv.reshape(1, d), sckv, shkv, wq, wk, wvt, *tables)


def _diff_kernel(lq1_ref, lk1_ref, lq2_ref, lk2_ref, g_ref, q_ref, k_ref, vt_ref, o_ref, acc_ref,
                 *, tq, lam_init):
    i = pl.program_id(2)
    lam = (jnp.exp(jnp.sum(lq1_ref[...] * lk1_ref[...], axis=-1, keepdims=True))
           - jnp.exp(jnp.sum(lq2_ref[...] * lk2_ref[...], axis=-1, keepdims=True)) + lam_init)
    q = q_ref[...]
    lane = lax.broadcasted_iota(jnp.int32, q.shape, 1)
    zero = jnp.zeros_like(q)
    q_maps = (jnp.where(lane < HEAD_DIM, q, zero), jnp.where(lane >= HEAD_DIM, q, zero))
    acc_ref[...] = jnp.zeros(acc_ref.shape, F32)
    key = lax.broadcasted_iota(jnp.int32, (tq, tq), 0)
    qry = lax.broadcasted_iota(jnp.int32, (tq, tq), 1)
    causal = key <= qry

    def block(jb, stats, masked):
        start = pl.multiple_of(jb * tq, tq)
        k = k_ref[pl.ds(start, tq), :]
        vt = vt_ref[jb]
        new_stats = []
        for c in range(2):
            m_old, l_old = stats[c]
            s = lax.dot_general(k, q_maps[c], (((1,), (1,)), ((), ())), preferred_element_type=F32)
            if masked:
                s = jnp.where(causal, s, NEG_INF)
            m_new = jnp.maximum(m_old, jnp.max(s, axis=0, keepdims=True))
            alpha = jnp.exp(m_old - m_new)
            p = jnp.exp(s - m_new)
            l_new = alpha * l_old + jnp.sum(p, axis=0, keepdims=True)
            acc_ref[c] = alpha * acc_ref[c] + jnp.dot(vt, p.astype(BF16), preferred_element_type=F32)
            new_stats.append((m_new, l_new))
        return tuple(new_stats)

    init = ((jnp.full((1, tq), NEG_INF, F32), jnp.zeros((1, tq), F32)),) * 2
    stats = lax.fori_loop(0, i, lambda jb, st: block(jb, st, False), init)
    stats = block(i, stats, True)
    o_t = acc_ref[0] * (1.0 / stats[0][1]) - lam * (acc_ref[1] * (1.0 / stats[1][1]))
    ms = jnp.mean(o_t * o_t, axis=0, keepdims=True)
    o_t = o_t * lax.rsqrt(ms + SUBLN_EPS) * g_ref[...] * (1.0 - lam_init)
    o_ref[...] = o_t.T.astype(BF16)


def _diff_attention(q, k, vt, lq1, lk1, lq2, lk2, subln, lam_init, tq):
    bsz, seq, d = q.shape
    width = 2 * HEAD_DIM
    vec = _resident((1, HEAD_DIM))
    k_spec = pl.BlockSpec((None, seq, width), lambda b, h, i: (b, 0, h))
    vt_spec = pl.BlockSpec((None, seq // tq, width, tq), lambda b, h, i: (b, 0, h, 0))
    qo_spec = pl.BlockSpec((None, tq, width), lambda b, h, i: (b, i, h))
    return pl.pallas_call(
        functools.partial(_diff_kernel, tq=tq, lam_init=lam_init),
        grid=(bsz, B_HEADS, seq // tq),
        in_specs=[vec, vec, vec, vec, _resident((width, 1)), qo_spec, k_spec, vt_spec],
        out_specs=qo_spec,
        out_shape=jax.ShapeDtypeStruct((bsz, seq, d), BF16),
        scratch_shapes=[pltpu.VMEM((2, width, tq), F32)],
        compiler_params=_params(3),
        name="diff_attn",
    )(lq1.reshape(1, HEAD_DIM), lk1.reshape(1, HEAD_DIM), lq2.reshape(1, HEAD_DIM),
      lk2.reshape(1, HEAD_DIM), subln.reshape(width, 1), q, k, vt)


def _out1_kernel(o_ref, x_ref, w_ref, g_ref, gate_ref, out_ref):
    y = jnp.dot(o_ref[...], w_ref[...], preferred_element_type=F32)
    out_ref[...] = x_ref[...] + gate_ref[...] * _rms(y, g_ref[...], NORM_EPS)


def _out1(o, x, w, gain, gate, tm=512):
    bsz, seq, d = x.shape
    spec = pl.BlockSpec((None, tm, d), lambda b, i: (b, i, 0))
    return pl.pallas_call(
        _out1_kernel,
        grid=(bsz, seq // tm),
        in_specs=[spec, spec, _resident((d, d)), _resident((1, d)),
                  pl.BlockSpec((None, 1, d), lambda b, i: (b, 0, 0))],
        out_specs=spec,
        out_shape=jax.ShapeDtypeStruct((bsz, seq, d), F32),
        compiler_params=_params(2),
        name="proj_out1",
    )(o, x, w, gain.reshape(1, d), gate)


def _split3(m):
    return jnp.split(m[:, None, :], 3, axis=-1)


def kernel(x, c, positions, mod_mix_w, mod_mix_b, mod_ffn_w, mod_ffn_b, norm_pre_mix, norm_post_mix, norm_pre_ffn, norm_post_ffn, ffn_w_gate, ffn_w_up, ffn_conv_w, ffn_conv_b, ffn_w_down, a_w_qkv, a_w_o, kv_norm, kv_mod_w, kv_mod_b, b_w_k, b_w_v, b_w_q, b_lambda_q1, b_lambda_k1, b_lambda_q2, b_lambda_k2, b_subln, b_w_o):
    mix_mod = _ada_mod(c, mod_mix_w, mod_mix_b)
    ffn_mod = _ada_mod(c, mod_ffn_w, mod_ffn_b)
    kv_mod = _ada_mod(c, kv_mod_w[None], kv_mod_b[None])[0]
    tables = _rope_tables(positions)

    def ffn(x, l):
        shift, scale, gate = _split3(ffn_mod[l])
        return _ffn(x, norm_pre_ffn[l], scale, shift, gate, ffn_w_gate[l].astype(BF16),
                    ffn_w_up[l].astype(BF16), ffn_conv_w[l], ffn_conv_b[l],
                    ffn_w_down[l].astype(BF16), norm_post_ffn[l])

    shift, scale, gate = _split3(mix_mod[0])
    qkv = _qkv0(x, norm_pre_mix[0], scale, shift, a_w_qkv[0].astype(BF16), tables)
    outs, lses = [], []
    for g, (_, dilation) in enumerate(A_GROUPS):
        o, lse = _dilated_attention(qkv, g, dilation)
        outs.append(o)
        lses.append(lse)
    x = _out0(outs, lses, x, a_w_o[0].astype(BF16), norm_post_mix[0], gate)
    x = ffn(x, 0)

    shift, scale, gate = _split3(mix_mod[1])
    kv_shift, kv_scale = jnp.split(kv_mod[:, None, :], 2, axis=-1)
    q, k, vt = _qkv1(x, norm_pre_mix[1], scale, shift, kv_norm, kv_scale, kv_shift,
                     b_w_q[0].astype(BF16), b_w_k.astype(BF16), b_w_v.T.astype(BF16), tables,
                     tm=DIFF_TILE)
    lam_init = 0.8 - 0.6 * math.exp(-0.3 * 1)
    o = _diff_attention(q, k, vt, b_lambda_q1[0], b_lambda_k1[0], b_lambda_q2[0], b_lambda_k2[0],
                        b_subln[0], lam_init, tq=DIFF_TILE)
    x = _out1(o, x, b_w_o[0].astype(BF16), norm_post_mix[1], gate)
    x = ffn(x, 1)
    return x
```

```python
import functools
import math

import jax
import jax.numpy as jnp
from jax import lax
from jax.experimental import pallas as pl
from jax.experimental.pallas import tpu as pltpu

D_MODEL = 1024
HEAD_DIM = 64
ROT_DIM = HEAD_DIM // 4
ROT_HALF = ROT_DIM // 2
ROPE_THETA = 500000.0
A_GROUPS = ((128, 1), (512, 4), (2048, 16))
A_HEADS = 8
A_INNER = A_HEADS * HEAD_DIM
B_HEADS = D_MODEL // (2 * HEAD_DIM)
CONV_WIDTH = 3
NORM_EPS = 1e-6
SUBLN_EPS = 1e-5
NEG_INF = -1e30
QK_SCALE = HEAD_DIM ** -0.5
LOG2E = math.log2(math.e)

LANES = 128
SUBLANES = 8
WIN = 128
PROJ_CHUNK = 512
VT_CHUNK = 256
DIFF_TILE = 512
DIL_TILES = ((512, 1), (256, 1), (128, 2))
VMEM_LIMIT = 56 * 1024 * 1024

F32 = jnp.float32
BF16 = jnp.bfloat16


def _resident(shape):
    zeros = (0,) * len(shape)
    return pl.BlockSpec(shape, lambda *_: zeros, pipeline_mode=pl.Buffered(1))


def _params(n_axes, sequential=False):
    sem = ("arbitrary",) * n_axes if sequential else ("parallel",) * n_axes
    return pltpu.CompilerParams(dimension_semantics=sem, vmem_limit_bytes=VMEM_LIMIT)


def _rms(x, g, eps):
    ms = jnp.mean(x * x, axis=-1, keepdims=True)
    return x * lax.rsqrt(ms + eps) * g


def _rope128(y, c, sa, sb):
    return y * c + pltpu.roll(y, LANES - ROT_HALF, 1) * sa + pltpu.roll(y, ROT_HALF, 1) * sb


def _mod_kernel(c_ref, w_ref, b_ref, o_ref):
    c = c_ref[...]
    c_act = c / (1.0 + jnp.exp(-c))
    o_ref[...] = jnp.dot(c_act, w_ref[...], preferred_element_type=F32) + b_ref[...]


def _ada_mod(c, w, b, tn=1024):
    n_layers, d, n = w.shape
    bsz = c.shape[0]
    return pl.pallas_call(
        _mod_kernel,
        grid=(n_layers, n // tn),
        in_specs=[
            pl.BlockSpec((bsz, d), lambda l, j: (0, 0)),
            pl.BlockSpec((None, d, tn), lambda l, j: (l, 0, j)),
            pl.BlockSpec((None, 1, tn), lambda l, j: (l, 0, j)),
        ],
        out_specs=pl.BlockSpec((None, bsz, tn), lambda l, j: (l, 0, j)),
        out_shape=jax.ShapeDtypeStruct((n_layers, bsz, n), F32),
        compiler_params=_params(2),
        name="ada_mod",
    )(c, w, b.reshape(n_layers, 1, n))


def _rope_kernel(pos_ref, inv_ref, c_ref, sa_ref, sb_ref):
    ang = pos_ref[...].astype(F32) * inv_ref[...]
    cos = jnp.cos(ang)
    sin = jnp.sin(ang)
    lane = lax.broadcasted_iota(jnp.int32, ang.shape, 1) & (HEAD_DIM - 1)
    c_ref[...] = cos
    sa_ref[...] = jnp.where(lane < ROT_HALF, -sin, 0.0)
    sb_ref[...] = jnp.where((lane >= ROT_HALF) & (lane < ROT_DIM), sin, 0.0)


def _rope_tables(positions, ts=1024):
    bsz, seq = positions.shape
    inv = jnp.power(ROPE_THETA, -jnp.arange(0, ROT_DIM, 2, dtype=F32) / ROT_DIM)
    head = jnp.concatenate([inv, inv, jnp.zeros((HEAD_DIM - ROT_DIM,), F32)])
    inv_lane = jnp.tile(head, LANES // HEAD_DIM).reshape(1, LANES)
    spec = pl.BlockSpec((None, ts, LANES), lambda b, i: (b, i, 0))
    shape = jax.ShapeDtypeStruct((bsz, seq, LANES), F32)
    return pl.pallas_call(
        _rope_kernel,
        grid=(bsz, seq // ts),
        in_specs=[pl.BlockSpec((None, ts, 1), lambda b, i: (b, i, 0)),
                  pl.BlockSpec((1, LANES), lambda b, i: (0, 0))],
        out_specs=[spec, spec, spec],
        out_shape=[shape, shape, shape],
        compiler_params=_params(2),
        name="rope_tables",
    )(positions.reshape(bsz, seq, 1), inv_lane)


def _qkv0_kernel(x_ref, g_ref, sc_ref, sh_ref, w_ref, c_ref, sa_ref, sb_ref, *rest, tm):
    out_refs, stage_ref = rest[:-1], rest[-1]
    h = _rms(x_ref[...], g_ref[...], NORM_EPS) * (1.0 + sc_ref[...]) + sh_ref[...]
    hb = h.astype(BF16)
    c, sa, sb = c_ref[...], sa_ref[...], sb_ref[...]
    slabs = A_INNER // LANES
    stage = 0
    for grp, (_, dil) in enumerate(A_GROUPS):
        o_ref = out_refs[grp]
        for kind in range(3):
            lo = (grp * 3 + kind) * A_INNER
            y = jnp.dot(hb, w_ref[:, lo:lo + A_INNER], preferred_element_type=F32)
            for j in range(slabs):
                yj = y[:, j * LANES:(j + 1) * LANES]
                if kind < 2:
                    yj = _rope128(yj, c, sa, sb)
                if kind == 0:
                    yj = yj * QK_SCALE
                cols = slice(kind * A_INNER + j * LANES, kind * A_INNER + (j + 1) * LANES)
                if dil == 1:
                    o_ref[0, :, cols] = yj.astype(BF16)
                    continue
                stage_ref[stage] = yj
                for r in range(dil):
                    rows = stage_ref[stage, pl.ds(r, tm // dil, stride=dil), :]
                    o_ref[r, :, cols] = rows.astype(BF16)
                stage += 1


def _qkv0(x, gain, scale, shift, w, tables, tm=512):
    bsz, seq, d = x.shape
    n = w.shape[1]
    row = pl.BlockSpec((None, 1, d), lambda b, i: (b, 0, 0))
    tab = pl.BlockSpec((None, tm, LANES), lambda b, i: (b, i, 0))
    width = 3 * A_INNER
    dils = [dil for _, dil in A_GROUPS]
    n_stage = sum(3 * (A_INNER // LANES) for dil in dils if dil > 1)
    return pl.pallas_call(
        functools.partial(_qkv0_kernel, tm=tm),
        grid=(bsz, seq // tm),
        in_specs=[pl.BlockSpec((None, tm, d), lambda b, i: (b, i, 0)),
                  _resident((1, d)), row, row, _resident((d, n)), tab, tab, tab],
        out_specs=[pl.BlockSpec((None, dil, tm // dil, width), lambda b, i: (b, 0, i, 0)) for dil in dils],
        out_shape=[jax.ShapeDtypeStruct((bsz, dil, seq // dil, width), BF16) for dil in dils],
        scratch_shapes=[pltpu.VMEM((n_stage, tm, LANES), F32)],
        compiler_params=_params(2),
        name="qkv0",
    )(x, gain.reshape(1, d), scale, shift, w, *tables)


def _dil_kernel(q_ref, kp_ref, k_ref, vp_ref, v_ref, o_ref, lse_ref, *, tq, dil, unroll):
    first_tile = pl.program_id(1) == 0
    row = lax.broadcasted_iota(jnp.int32, (2 * WIN, 2 * WIN), 0) & (WIN - 1)
    col = lax.broadcasted_iota(jnp.int32, (2 * WIN, 2 * WIN), 1)
    band = (col >= row) & (col <= row + WIN)
    band_first = band & ((col >= WIN) | jnp.logical_not(first_tile))
    lane = lax.broadcasted_iota(jnp.int32, (WIN, LANES), 1)
    low_q = lane < HEAD_DIM
    low_kv = lax.broadcasted_iota(jnp.int32, (2 * WIN, LANES), 1) < HEAD_DIM

    def residue(r):
        for j in range(tq // WIN):
            rows = slice(j * WIN, (j + 1) * WIN)
            if dil == 1:
                out_rows = rows
            else:
                out_rows = pl.ds(j * WIN * dil + r, WIN, stride=dil)
            lse_acc = jnp.zeros((WIN, LANES), F32)
            for hp in range(A_HEADS // 2):
                cols = slice(hp * LANES, (hp + 1) * LANES)
                q = q_ref[r, rows, cols]
                if j == 0:
                    kk = jnp.concatenate([kp_ref[r, :, cols], k_ref[r, 0:WIN, cols]], axis=0)
                    vv = jnp.concatenate([vp_ref[r, :, cols], v_ref[r, 0:WIN, cols]], axis=0)
                else:
                    kk = k_ref[r, (j - 1) * WIN:(j + 1) * WIN, cols]
                    vv = v_ref[r, (j - 1) * WIN:(j + 1) * WIN, cols]
                zq, zv = jnp.zeros_like(q), jnp.zeros_like(vv)
                q2 = jnp.concatenate([jnp.where(low_q, q, zq), jnp.where(low_q, zq, q)], axis=0)
                s = lax.dot_general(q2, kk, (((1,), (1,)), ((), ())), preferred_element_type=F32)
                s = jnp.where(band_first if j == 0 else band, s, NEG_INF)
                mx = jnp.max(s, axis=-1, keepdims=True)
                p = jnp.exp(s - mx)
                den = jnp.sum(p, axis=-1, keepdims=True)
                pb = p.astype(BF16)
                p2 = jnp.concatenate([pb[0:WIN], pb[WIN:2 * WIN]], axis=1)
                v2 = jnp.concatenate([jnp.where(low_kv, vv, zv), jnp.where(low_kv, zv, vv)], axis=0)
                o = jnp.dot(p2, v2, preferred_element_type=F32)
                inv = 1.0 / den
                o_ref[hp, out_rows, :] = o * jnp.where(low_q, inv[0:WIN], inv[WIN:2 * WIN])
                lse = mx + jnp.log(den)
                lse_acc = jnp.where(lane == 2 * hp, lse[0:WIN], lse_acc)
                lse_acc = jnp.where(lane == 2 * hp + 1, lse[WIN:2 * WIN], lse_acc)
            lse_ref[out_rows, :] = lse_acc

    if dil == 1:
        residue(0)
    else:
        def body(t, carry):
            for u in range(unroll):
                residue(t * unroll + u)
            return carry
        lax.fori_loop(0, dil // unroll, body, 0)


def _dilated_attention(qkv, dilation, tq, unroll=1):
    bsz, _, length, width = qkv.shape
    seq = length * dilation
    sub = tq // WIN
    span = tq * dilation

    def cur(kind):
        return pl.BlockSpec((None, dilation, tq, A_INNER), lambda b, i: (b, 0, i, kind))

    def prev(kind):
        return pl.BlockSpec((None, dilation, WIN, A_INNER),
                            lambda b, i: (b, 0, jnp.maximum(i * sub - 1, 0), kind))

    return pl.pallas_call(
        functools.partial(_dil_kernel, tq=tq, dil=dilation, unroll=unroll),
        grid=(bsz, length // tq),
        in_specs=[cur(0), prev(1), cur(1), prev(2), cur(2)],
        out_specs=[pl.BlockSpec((None, A_INNER // LANES, span, LANES), lambda b, i: (b, 0, i, 0)),
                   pl.BlockSpec((None, span, LANES), lambda b, i: (b, i, 0))],
        out_shape=[jax.ShapeDtypeStruct((bsz, A_INNER // LANES, seq, LANES), F32),
                   jax.ShapeDtypeStruct((bsz, seq, LANES), F32)],
        compiler_params=_params(2),
        name=f"dilated_attn_d{dilation}",
    )(qkv, qkv, qkv, qkv, qkv)


def _out0_kernel(o0_ref, o1_ref, o2_ref, l0_ref, l1_ref, l2_ref, x_ref, w_ref, g_ref, gate_ref, out_ref):
    l0, l1, l2 = l0_ref[...], l1_ref[...], l2_ref[...]
    m = jnp.maximum(jnp.maximum(l0, l1), l2)
    e0, e1, e2 = jnp.exp(l0 - m), jnp.exp(l1 - m), jnp.exp(l2 - m)
    inv = 1.0 / (e0 + e1 + e2)
    alphas = (e0 * inv, e1 * inv, e2 * inv)
    o_refs = (o0_ref, o1_ref, o2_ref)
    lane = lax.broadcasted_iota(jnp.int32, l0.shape, 1)
    slabs = []
    for hp in range(A_HEADS // 2):
        mixed = None
        for a, o_ref in zip(alphas, o_refs):
            wgt = jnp.where(lane < HEAD_DIM, a[:, 2 * hp:2 * hp + 1], a[:, 2 * hp + 1:2 * hp + 2])
            term = wgt * o_ref[hp]
            mixed = term if mixed is None else mixed + term
        slabs.append(mixed.astype(BF16))
    y = jnp.dot(jnp.concatenate(slabs, axis=1), w_ref[...], preferred_element_type=F32)
    out_ref[...] = x_ref[...] + gate_ref[...] * _rms(y, g_ref[...], NORM_EPS)


def _out0(outs, lses, x, w, gain, gate, tm=512):
    bsz, seq, d = x.shape
    o_spec = pl.BlockSpec((None, A_INNER // LANES, tm, LANES), lambda b, i: (b, 0, i, 0))
    l_spec = pl.BlockSpec((None, tm, LANES), lambda b, i: (b, i, 0))
    x_spec = pl.BlockSpec((None, tm, d), lambda b, i: (b, i, 0))
    return pl.pallas_call(
        _out0_kernel,
        grid=(bsz, seq // tm),
        in_specs=[o_spec] * 3 + [l_spec] * 3 + [
            x_spec, _resident((A_INNER, d)), _resident((1, d)),
            pl.BlockSpec((None, 1, d), lambda b, i: (b, 0, 0))],
        out_specs=x_spec,
        out_shape=jax.ShapeDtypeStruct((bsz, seq, d), F32),
        compiler_params=_params(2),
        name="mix_out0",
    )(*outs, *lses, x, w, gain.reshape(1, d), gate)


def _ffn_kernel(x_ref, gpre_ref, sc_ref, sh_ref, gate_ref, wg_ref, wu_ref, cw_ref, cb_ref, wd_ref,
                gpost_ref, out_ref, a_buf, *, tm, fc):
    @pl.when(pl.program_id(1) == 0)
    def _():
        a_buf[0:SUBLANES, :] = jnp.zeros((SUBLANES, a_buf.shape[1]), F32)

    x = x_ref[...]
    h = _rms(x, gpre_ref[...], NORM_EPS) * (1.0 + sc_ref[...]) + sh_ref[...]
    hb = h.astype(BF16)
    d_ff = wg_ref.shape[1]
    y = None
    for lo in range(0, d_ff, fc):
        cols = slice(lo, lo + fc)
        a_buf[SUBLANES:SUBLANES + tm, cols] = jnp.dot(hb, wg_ref[:, cols], preferred_element_type=F32)
        u = jnp.dot(hb, wu_ref[:, cols], preferred_element_type=F32)
        conv = cb_ref[:, cols]
        for t in range(CONV_WIDTH):
            start = SUBLANES - (CONV_WIDTH - 1) + t
            conv = conv + cw_ref[t:t + 1, cols] * a_buf[start:start + tm, cols]
        inner = math.sqrt(2.0 / math.pi) * (conv + 0.044715 * (conv * conv * conv))
        act = 0.5 * conv * (1.0 + jnp.tanh(inner)) * u
        part = jnp.dot(act.astype(BF16), wd_ref[cols, :], preferred_element_type=F32)
        y = part if y is None else y + part
        a_buf[0:SUBLANES, cols] = a_buf[tm:tm + SUBLANES, cols]
    out_ref[...] = x + gate_ref[...] * _rms(y, gpost_ref[...], NORM_EPS)


def _ffn(x, gpre, scale, shift, gate, wg, wu, cw, cb, wd, gpost, tm=512, fc=256):
    bsz, seq, d = x.shape
    d_ff = wg.shape[1]
    x_spec = pl.BlockSpec((None, tm, d), lambda b, i: (b, i, 0))
    row = pl.BlockSpec((None, 1, d), lambda b, i: (b, 0, 0))
    return pl.pallas_call(
        functools.partial(_ffn_kernel, tm=tm, fc=fc),
        grid=(bsz, seq // tm),
        in_specs=[x_spec, _resident((1, d)), row, row, row,
                  _resident((d, d_ff)), _resident((d, d_ff)), _resident((CONV_WIDTH, d_ff)),
                  _resident((1, d_ff)), _resident((d_ff, d)), _resident((1, d))],
        out_specs=x_spec,
        out_shape=jax.ShapeDtypeStruct((bsz, seq, d), F32),
        scratch_shapes=[pltpu.VMEM((tm + SUBLANES, d_ff), F32)],
        compiler_params=_params(2, sequential=True),
        name="conv_ffn",
    )(x, gpre.reshape(1, d), scale, shift, gate, wg, wu, cw, cb.reshape(1, d_ff), wd, gpost.reshape(1, d))


def _qkv1_kernel(x_ref, gq_ref, scq_ref, shq_ref, gkv_ref, sckv_ref, shkv_ref, wq_ref, wk_ref, wvt_ref,
                 c_ref, sa_ref, sb_ref, q_ref, k_ref, vt_ref):
    x = x_ref[...]
    rstd = lax.rsqrt(jnp.mean(x * x, axis=-1, keepdims=True) + NORM_EPS)
    xn = x * rstd
    hq = ((xn * gq_ref[...]) * (1.0 + scq_ref[...]) + shq_ref[...]).astype(BF16)
    hkv = ((xn * gkv_ref[...]) * (1.0 + sckv_ref[...]) + shkv_ref[...]).astype(BF16)
    c, sa, sb = c_ref[...], sa_ref[...], sb_ref[...]
    d = x.shape[1]
    for lo in range(0, d, PROJ_CHUNK):
        yq = jnp.dot(hq, wq_ref[:, lo:lo + PROJ_CHUNK], preferred_element_type=F32)
        yk = jnp.dot(hkv, wk_ref[:, lo:lo + PROJ_CHUNK], preferred_element_type=F32)
        for j in range(PROJ_CHUNK // LANES):
            cols = slice(lo + j * LANES, lo + (j + 1) * LANES)
            sub = slice(j * LANES, (j + 1) * LANES)
            q_ref[:, cols] = (_rope128(yq[:, sub], c, sa, sb) * (QK_SCALE * LOG2E)).astype(BF16)
            k_ref[:, cols] = _rope128(yk[:, sub], c, sa, sb).astype(BF16)
    for lo in range(0, d, VT_CHUNK):
        vt = lax.dot_general(wvt_ref[lo:lo + VT_CHUNK, :], hkv, (((1,), (1,)), ((), ())),
                             preferred_element_type=F32)
        vt_ref[lo:lo + VT_CHUNK, :] = vt.astype(BF16)


def _qkv1(x, gq, scq, shq, gkv, sckv, shkv, wq, wk, wvt, tables, tm):
    bsz, seq, d = x.shape
    x_spec = pl.BlockSpec((None, tm, d), lambda b, i: (b, i, 0))
    row = pl.BlockSpec((None, 1, d), lambda b, i: (b, 0, 0))
    tab = pl.BlockSpec((None, tm, LANES), lambda b, i: (b, i, 0))
    shape = jax.ShapeDtypeStruct((bsz, seq, d), BF16)
    return pl.pallas_call(
        _qkv1_kernel,
        grid=(bsz, seq // tm),
        in_specs=[x_spec, _resident((1, d)), row, row, _resident((1, d)), row, row,
                  _resident((d, d)), _resident((d, d)), _resident((d, d)), tab, tab, tab],
        out_specs=[x_spec, x_spec, pl.BlockSpec((None, None, d, tm), lambda b, i: (b, i, 0, 0))],
        out_shape=[shape, shape, jax.ShapeDtypeStruct((bsz, seq // tm, d, tm), BF16)],
        compiler_params=_params(2),
        name="qkv1",
    )(x, gq.reshape(1, d), scq, shq, gkv.reshape(1, d), sckv, shkv, wq, wk, wvt, *tables)


def _diff_kernel(lq1_ref, lk1_ref, lq2_ref, lk2_ref, g_ref, q_ref, k_ref, vt_ref, o_ref,
                 acc_ref, m_ref, l_ref, s_ref, *, tq, lam_init):
    i = pl.program_id(2)
    lam = (jnp.exp(jnp.sum(lq1_ref[...] * lk1_ref[...], axis=-1, keepdims=True))
           - jnp.exp(jnp.sum(lq2_ref[...] * lk2_ref[...], axis=-1, keepdims=True)) + lam_init)
    q = q_ref[...]
    lane = lax.broadcasted_iota(jnp.int32, q.shape, 1)
    zero = jnp.zeros_like(q)
    q_maps = (jnp.where(lane < HEAD_DIM, q, zero), jnp.where(lane >= HEAD_DIM, q, zero))
    acc_ref[...] = jnp.zeros(acc_ref.shape, F32)
    m_ref[...] = jnp.full(m_ref.shape, NEG_INF, F32)
    l_ref[...] = jnp.zeros(l_ref.shape, F32)

    def scores(jb, slot, masked):
        start = jb * tq
        if not isinstance(jb, int):
            start = pl.multiple_of(start, tq)
        k = k_ref[pl.ds(start, tq), :]
        for c in range(2):
            s = lax.dot_general(k, q_maps[c], (((1,), (1,)), ((), ())), preferred_element_type=F32)
            if masked:
                key = lax.broadcasted_iota(jnp.int32, (tq, tq), 0)
                qry = lax.broadcasted_iota(jnp.int32, (tq, tq), 1)
                s = jnp.where(key <= qry, s, NEG_INF)
            s_ref[slot, c] = s

    def update(jb, slot):
        vt = vt_ref[jb]
        for c in range(2):
            s = s_ref[slot, c]
            m_old = m_ref[c]
            m_new = jnp.maximum(m_old, jnp.max(s, axis=0, keepdims=True))
            alpha = jnp.exp2(m_old - m_new)
            p = jnp.exp2(s - m_new)
            l_ref[c] = alpha * l_ref[c] + jnp.sum(p, axis=0, keepdims=True)
            acc_ref[c] = alpha * acc_ref[c] + jnp.dot(vt, p.astype(BF16), preferred_element_type=F32)
            m_ref[c] = m_new

    def pair(jb, src, dst, masked):
        scores(jb + 1, dst, masked)
        update(jb, src)

    odd = (i & 1) == 1

    @pl.when(i == 0)
    def _():
        scores(0, 0, True)
        update(0, 0)

    @pl.when(i > 0)
    def _():
        scores(0, 0, False)

    def body(t, carry):
        pair(2 * t, 0, 1, False)
        pair(2 * t + 1, 1, 0, False)
        return carry

    lax.fori_loop(0, lax.shift_right_arithmetic(i - 1, 1), body, 0)

    @pl.when(odd)
    def _():
        pair(i - 1, 0, 1, True)
        update(i, 1)

    @pl.when(jnp.logical_and(i > 0, jnp.logical_not(odd)))
    def _():
        pair(i - 2, 0, 1, False)
        pair(i - 1, 1, 0, True)
        update(i, 0)

    o_t = acc_ref[0] * (1.0 / l_ref[0]) - lam * (acc_ref[1] * (1.0 / l_ref[1]))
    ms = jnp.mean(o_t * o_t, axis=0, keepdims=True)
    o_t = o_t * lax.rsqrt(ms + SUBLN_EPS) * g_ref[...] * (1.0 - lam_init)
    o_ref[...] = o_t.T.astype(BF16)


def _diff_attention(q, k, vt, lq1, lk1, lq2, lk2, subln, lam_init, tq):
    bsz, seq, d = q.shape
    width = 2 * HEAD_DIM
    vec = _resident((1, HEAD_DIM))
    k_spec = pl.BlockSpec((None, seq, width), lambda b, h, i: (b, 0, h))
    vt_spec = pl.BlockSpec((None, seq // tq, width, tq), lambda b, h, i: (b, 0, h, 0))
    qo_spec = pl.BlockSpec((None, tq, width), lambda b, h, i: (b, i, h))
    return pl.pallas_call(
        functools.partial(_diff_kernel, tq=tq, lam_init=lam_init),
        grid=(bsz, B_HEADS, seq // tq),
        in_specs=[vec, vec, vec, vec, _resident((width, 1)), qo_spec, k_spec, vt_spec],
        out_specs=qo_spec,
        out_shape=jax.ShapeDtypeStruct((bsz, seq, d), BF16),
        scratch_shapes=[pltpu.VMEM((2, width, tq), F32), pltpu.VMEM((2, 1, tq), F32),
                        pltpu.VMEM((2, 1, tq), F32), pltpu.VMEM((2, 2, tq, tq), F32)],
        compiler_params=_params(3),
        name="diff_attn",
    )(lq1.reshape(1, HEAD_DIM), lk1.reshape(1, HEAD_DIM), lq2.reshape(1, HEAD_DIM),
      lk2.reshape(1, HEAD_DIM), subln.reshape(width, 1), q, k, vt)


def _out1_kernel(o_ref, x_ref, w_ref, g_ref, gate_ref, out_ref):
    y = jnp.dot(o_ref[...], w_ref[...], preferred_element_type=F32)
    out_ref[...] = x_ref[...] + gate_ref[...] * _rms(y, g_ref[...], NORM_EPS)


def _out1(o, x, w, gain, gate, tm=512):
    bsz, seq, d = x.shape
    spec = pl.BlockSpec((None, tm, d), lambda b, i: (b, i, 0))
    return pl.pallas_call(
        _out1_kernel,
        grid=(bsz, seq // tm),
        in_specs=[spec, spec, _resident((d, d)), _resident((1, d)),
                  pl.BlockSpec((None, 1, d), lambda b, i: (b, 0, 0))],
        out_specs=spec,
        out_shape=jax.ShapeDtypeStruct((bsz, seq, d), F32),
        compiler_params=_params(2),
        name="proj_out1",
    )(o, x, w, gain.reshape(1, d), gate)


def _split3(m):
    return jnp.split(m[:, None, :], 3, axis=-1)


def kernel(x, c, positions, mod_mix_w, mod_mix_b, mod_ffn_w, mod_ffn_b, norm_pre_mix, norm_post_mix, norm_pre_ffn, norm_post_ffn, ffn_w_gate, ffn_w_up, ffn_conv_w, ffn_conv_b, ffn_w_down, a_w_qkv, a_w_o, kv_norm, kv_mod_w, kv_mod_b, b_w_k, b_w_v, b_w_q, b_lambda_q1, b_lambda_k1, b_lambda_q2, b_lambda_k2, b_subln, b_w_o):
    mix_mod = _ada_mod(c, mod_mix_w, mod_mix_b)
    ffn_mod = _ada_mod(c, mod_ffn_w, mod_ffn_b)
    kv_mod = _ada_mod(c, kv_mod_w[None], kv_mod_b[None])[0]
    tables = _rope_tables(positions)

    def ffn(x, l):
        shift, scale, gate = _split3(ffn_mod[l])
        return _ffn(x, norm_pre_ffn[l], scale, shift, gate, ffn_w_gate[l].astype(BF16),
                    ffn_w_up[l].astype(BF16), ffn_conv_w[l], ffn_conv_b[l],
                    ffn_w_down[l].astype(BF16), norm_post_ffn[l])

    shift, scale, gate = _split3(mix_mod[0])
    qkv_groups = _qkv0(x, norm_pre_mix[0], scale, shift, a_w_qkv[0].astype(BF16), tables)
    outs, lses = [], []
    for qkv, (_, dilation), (tq, unroll) in zip(qkv_groups, A_GROUPS, DIL_TILES):
        o, lse = _dilated_attention(qkv, dilation, tq, unroll)
        outs.append(o)
        lses.append(lse)
    x = _out0(outs, lses, x, a_w_o[0].astype(BF16), norm_post_mix[0], gate)
    x = ffn(x, 0)

    shift, scale, gate = _split3(mix_mod[1])
    kv_shift, kv_scale = jnp.split(kv_mod[:, None, :], 2, axis=-1)
    q, k, vt = _qkv1(x, norm_pre_mix[1], scale, shift, kv_norm, kv_scale, kv_shift,
                     b_w_q[0].astype(BF16), b_w_k.astype(BF16), b_w_v.T.astype(BF16), tables,
                     tm=DIFF_TILE)
    lam_init = 0.8 - 0.6 * math.exp(-0.3 * 1)
    o = _diff_attention(q, k, vt, b_lambda_q1[0], b_lambda_k1[0], b_lambda_q2[0], b_lambda_k2[0],
                        b_subln[0], lam_init, tq=DIFF_TILE)
    x = _out1(o, x, b_w_o[0].astype(BF16), norm_post_mix[1], gate)
    x = ffn(x, 1)
    return x
```

```python
import functools
import math

import jax
import jax.numpy as jnp
from jax import lax
from jax.experimental import pallas as pl
from jax.experimental.pallas import tpu as pltpu

D_MODEL = 1024
HEAD_DIM = 64
ROT_DIM = HEAD_DIM // 4
ROT_HALF = ROT_DIM // 2
ROPE_THETA = 500000.0
A_GROUPS = ((128, 1), (512, 4), (2048, 16))
A_HEADS = 8
A_INNER = A_HEADS * HEAD_DIM
B_HEADS = D_MODEL // (2 * HEAD_DIM)
CONV_WIDTH = 3
NORM_EPS = 1e-6
SUBLN_EPS = 1e-5
NEG_INF = -1e30
QK_SCALE = HEAD_DIM ** -0.5
LOG2E = math.log2(math.e)
GELU_C1 = math.sqrt(2.0 / math.pi)
GELU_C3 = GELU_C1 * 0.044715

LANES = 128
SUBLANES = 8
WIN = 128
PROJ_CHUNK = 512
VT_CHUNK = 256
FFN_CHUNK = 256
DIFF_TILE = 512
DIL_TILES = ((512, 1), (256, 1), (128, 2))
VMEM_LIMIT = 56 * 1024 * 1024

F32 = jnp.float32
BF16 = jnp.bfloat16


def _resident(shape):
    zeros = (0,) * len(shape)
    return pl.BlockSpec(shape, lambda *_: zeros, pipeline_mode=pl.Buffered(1))


def _params(n_axes, sequential=False):
    sem = ("arbitrary",) * n_axes if sequential else ("parallel",) * n_axes
    return pltpu.CompilerParams(dimension_semantics=sem, vmem_limit_bytes=VMEM_LIMIT)


def _rms(x, g, eps):
    ms = jnp.mean(x * x, axis=-1, keepdims=True)
    return x * lax.rsqrt(ms + eps) * g


def _rope128(y, c, sa, sb):
    return y * c + pltpu.roll(y, LANES - ROT_HALF, 1) * sa + pltpu.roll(y, ROT_HALF, 1) * sb


def _mod_kernel(c_ref, w_ref, b_ref, o_ref):
    c = c_ref[...]
    c_act = c / (1.0 + jnp.exp(-c))
    o_ref[...] = jnp.dot(c_act, w_ref[...], preferred_element_type=F32) + b_ref[...]


def _ada_mod(c, w, b, tn=1024):
    n_layers, d, n = w.shape
    bsz = c.shape[0]
    return pl.pallas_call(
        _mod_kernel,
        grid=(n_layers, n // tn),
        in_specs=[
            pl.BlockSpec((bsz, d), lambda l, j: (0, 0)),
            pl.BlockSpec((None, d, tn), lambda l, j: (l, 0, j)),
            pl.BlockSpec((None, 1, tn), lambda l, j: (l, 0, j)),
        ],
        out_specs=pl.BlockSpec((None, bsz, tn), lambda l, j: (l, 0, j)),
        out_shape=jax.ShapeDtypeStruct((n_layers, bsz, n), F32),
        compiler_params=_params(2),
        name="ada_mod",
    )(c, w, b.reshape(n_layers, 1, n))


def _rope_kernel(pos_ref, inv_ref, c_ref, sa_ref, sb_ref):
    ang = pos_ref[...].astype(F32) * inv_ref[...]
    cos = jnp.cos(ang)
    sin = jnp.sin(ang)
    lane = lax.broadcasted_iota(jnp.int32, ang.shape, 1) & (HEAD_DIM - 1)
    c_ref[...] = cos
    sa_ref[...] = jnp.where(lane < ROT_HALF, -sin, 0.0)
    sb_ref[...] = jnp.where((lane >= ROT_HALF) & (lane < ROT_DIM), sin, 0.0)


def _rope_tables(positions, ts=1024):
    bsz, seq = positions.shape
    inv = jnp.power(ROPE_THETA, -jnp.arange(0, ROT_DIM, 2, dtype=F32) / ROT_DIM)
    head = jnp.concatenate([inv, inv, jnp.zeros((HEAD_DIM - ROT_DIM,), F32)])
    inv_lane = jnp.tile(head, LANES // HEAD_DIM).reshape(1, LANES)
    spec = pl.BlockSpec((None, ts, LANES), lambda b, i: (b, i, 0))
    shape = jax.ShapeDtypeStruct((bsz, seq, LANES), F32)
    return pl.pallas_call(
        _rope_kernel,
        grid=(bsz, seq // ts),
        in_specs=[pl.BlockSpec((None, ts, 1), lambda b, i: (b, i, 0)),
                  pl.BlockSpec((1, LANES), lambda b, i: (0, 0))],
        out_specs=[spec, spec, spec],
        out_shape=[shape, shape, shape],
        compiler_params=_params(2),
        name="rope_tables",
    )(positions.reshape(bsz, seq, 1), inv_lane)


def _qkv0_kernel(x_ref, g_ref, sc_ref, sh_ref, w_ref, c_ref, sa_ref, sb_ref, *rest, tm):
    out_refs, stage_ref = rest[:-1], rest[-1]
    h = _rms(x_ref[...], g_ref[...], NORM_EPS) * (1.0 + sc_ref[...]) + sh_ref[...]
    hb = h.astype(BF16)
    c, sa, sb = c_ref[...], sa_ref[...], sb_ref[...]
    slabs = A_INNER // LANES
    stage = 0
    for grp, (_, dil) in enumerate(A_GROUPS):
        o_ref = out_refs[grp]
        for kind in range(3):
            lo = (grp * 3 + kind) * A_INNER
            y = jnp.dot(hb, w_ref[:, lo:lo + A_INNER], preferred_element_type=F32)
            for j in range(slabs):
                yj = y[:, j * LANES:(j + 1) * LANES]
                if kind < 2:
                    yj = _rope128(yj, c, sa, sb)
                if kind == 0:
                    yj = yj * QK_SCALE
                cols = slice(kind * A_INNER + j * LANES, kind * A_INNER + (j + 1) * LANES)
                if dil == 1:
                    o_ref[0, :, cols] = yj.astype(BF16)
                    continue
                stage_ref[stage] = yj
                for r in range(dil):
                    rows = stage_ref[stage, pl.ds(r, tm // dil, stride=dil), :]
                    o_ref[r, :, cols] = rows.astype(BF16)
                stage += 1


def _qkv0(x, gain, scale, shift, w, tables, tm=512):
    bsz, seq, d = x.shape
    n = w.shape[1]
    row = pl.BlockSpec((None, 1, d), lambda b, i: (b, 0, 0))
    tab = pl.BlockSpec((None, tm, LANES), lambda b, i: (b, i, 0))
    width = 3 * A_INNER
    dils = [dil for _, dil in A_GROUPS]
    n_stage = sum(3 * (A_INNER // LANES) for dil in dils if dil > 1)
    return pl.pallas_call(
        functools.partial(_qkv0_kernel, tm=tm),
        grid=(bsz, seq // tm),
        in_specs=[pl.BlockSpec((None, tm, d), lambda b, i: (b, i, 0)),
                  _resident((1, d)), row, row, _resident((d, n)), tab, tab, tab],
        out_specs=[pl.BlockSpec((None, dil, tm // dil, width), lambda b, i: (b, 0, i, 0)) for dil in dils],
        out_shape=[jax.ShapeDtypeStruct((bsz, dil, seq // dil, width), BF16) for dil in dils],
        scratch_shapes=[pltpu.VMEM((n_stage, tm, LANES), F32)],
        compiler_params=_params(2),
        name="qkv0",
    )(x, gain.reshape(1, d), scale, shift, w, *tables)


def _dil_kernel(q_ref, kp_ref, k_ref, vp_ref, v_ref, o_ref, lse_ref, *, tq, dil, unroll):
    first_tile = pl.program_id(1) == 0
    row = lax.broadcasted_iota(jnp.int32, (2 * WIN, 2 * WIN), 0) & (WIN - 1)
    col = lax.broadcasted_iota(jnp.int32, (2 * WIN, 2 * WIN), 1)
    band = (col >= row) & (col <= row + WIN)
    band_first = band & ((col >= WIN) | jnp.logical_not(first_tile))
    lane = lax.broadcasted_iota(jnp.int32, (WIN, LANES), 1)
    low_q = lane < HEAD_DIM
    low_kv = lax.broadcasted_iota(jnp.int32, (2 * WIN, LANES), 1) < HEAD_DIM

    def residue(r):
        for j in range(tq // WIN):
            rows = slice(j * WIN, (j + 1) * WIN)
            if dil == 1:
                out_rows = rows
            else:
                out_rows = pl.ds(j * WIN * dil + r, WIN, stride=dil)
            for hp in range(A_HEADS // 2):
                cols = slice(hp * LANES, (hp + 1) * LANES)
                q = q_ref[r, rows, cols]
                if j == 0:
                    kk = jnp.concatenate([kp_ref[r, :, cols], k_ref[r, 0:WIN, cols]], axis=0)
                    vv = jnp.concatenate([vp_ref[r, :, cols], v_ref[r, 0:WIN, cols]], axis=0)
                else:
                    kk = k_ref[r, (j - 1) * WIN:(j + 1) * WIN, cols]
                    vv = v_ref[r, (j - 1) * WIN:(j + 1) * WIN, cols]
                zq, zv = jnp.zeros_like(q), jnp.zeros_like(vv)
                q2 = jnp.concatenate([jnp.where(low_q, q, zq), jnp.where(low_q, zq, q)], axis=0)
                s = lax.dot_general(q2, kk, (((1,), (1,)), ((), ())), preferred_element_type=F32)
                s = jnp.where(band_first if j == 0 else band, s, NEG_INF)
                mx = jnp.max(s, axis=-1, keepdims=True)
                p = jnp.exp(s - mx)
                den = jnp.sum(p, axis=-1, keepdims=True)
                pb = p.astype(BF16)
                p2 = jnp.concatenate([pb[0:WIN], pb[WIN:2 * WIN]], axis=1)
                v2 = jnp.concatenate([jnp.where(low_kv, vv, zv), jnp.where(low_kv, zv, vv)], axis=0)
                o = jnp.dot(p2, v2, preferred_element_type=F32)
                inv = 1.0 / den
                o_ref[hp, out_rows, :] = o * jnp.where(low_q, inv[0:WIN], inv[WIN:2 * WIN])
                lse = mx + jnp.log(den)
                lse_ref[hp, out_rows, :] = jnp.where(low_q, lse[0:WIN], lse[WIN:2 * WIN])

    if dil == 1:
        residue(0)
    else:
        def body(t, carry):
            for u in range(unroll):
                residue(t * unroll + u)
            return carry
        lax.fori_loop(0, dil // unroll, body, 0)


def _dilated_attention(qkv, dilation, tq, unroll=1):
    bsz, _, length, width = qkv.shape
    seq = length * dilation
    sub = tq // WIN
    span = tq * dilation

    def cur(kind):
        return pl.BlockSpec((None, dilation, tq, A_INNER), lambda b, i: (b, 0, i, kind))

    def prev(kind):
        return pl.BlockSpec((None, dilation, WIN, A_INNER),
                            lambda b, i: (b, 0, jnp.maximum(i * sub - 1, 0), kind))

    return pl.pallas_call(
        functools.partial(_dil_kernel, tq=tq, dil=dilation, unroll=unroll),
        grid=(bsz, length // tq),
        in_specs=[cur(0), prev(1), cur(1), prev(2), cur(2)],
        out_specs=[pl.BlockSpec((None, A_INNER // LANES, span, LANES), lambda b, i: (b, 0, i, 0))] * 2,
        out_shape=[jax.ShapeDtypeStruct((bsz, A_INNER // LANES, seq, LANES), F32)] * 2,
        compiler_params=_params(2),
        name=f"dilated_attn_d{dilation}",
    )(qkv, qkv, qkv, qkv, qkv)


def _mixture(o_refs, l_refs):
    slabs = []
    for hp in range(A_HEADS // 2):
        lses = [l_ref[hp] for l_ref in l_refs]
        m = functools.reduce(jnp.maximum, lses)
        es = [jnp.exp(l - m) for l in lses]
        inv = 1.0 / functools.reduce(lambda a, b: a + b, es)
        mixed = functools.reduce(lambda a, b: a + b, [e * o_ref[hp] for e, o_ref in zip(es, o_refs)])
        slabs.append((mixed * inv).astype(BF16))
    return jnp.concatenate(slabs, axis=1)


def _tail_kernel(*refs, tm, fc, n_groups):
    n_attn = 2 * n_groups if n_groups else 1
    attn_refs = refs[:n_attn]
    (x_ref, wo_ref, gmix_ref, gatemix_ref, gpre_ref, sc_ref, sh_ref, gate_ref, wg_ref, wu_ref,
     cw_ref, cb_ref, wd_ref, gpost_ref, out_ref, a_buf, act_buf) = refs[n_attn:]
    n_chunks = a_buf.shape[0]

    @pl.when(pl.program_id(1) == 0)
    def _():
        for ch in range(n_chunks):
            a_buf[ch, 0:SUBLANES, :] = jnp.zeros((SUBLANES, fc), F32)

    if n_groups:
        attn = _mixture(attn_refs[:n_groups], attn_refs[n_groups:])
    else:
        attn = attn_refs[0][...]
    y_attn = jnp.dot(attn, wo_ref[...], preferred_element_type=F32)
    x = x_ref[...] + gatemix_ref[...] * _rms(y_attn, gmix_ref[...], NORM_EPS)
    h = _rms(x, gpre_ref[...], NORM_EPS) * (1.0 + sc_ref[...]) + sh_ref[...]
    hb = h.astype(BF16)

    def gate_up(ch):
        cols = slice(ch * fc, (ch + 1) * fc)
        a_buf[ch, SUBLANES:SUBLANES + tm, :] = jnp.dot(hb, wg_ref[:, cols], preferred_element_type=F32)
        return jnp.dot(hb, wu_ref[:, cols], preferred_element_type=F32)

    u_next = gate_up(0)
    for ch in range(n_chunks):
        cols = slice(ch * fc, (ch + 1) * fc)
        u = u_next
        if ch + 1 < n_chunks:
            u_next = gate_up(ch + 1)
        conv = cb_ref[:, cols]
        for t in range(CONV_WIDTH):
            start = SUBLANES - (CONV_WIDTH - 1) + t
            conv = conv + cw_ref[t:t + 1, cols] * a_buf[ch, start:start + tm, :]
        inner = conv * (GELU_C1 + GELU_C3 * (conv * conv))
        act = 0.5 * conv * (1.0 + jnp.tanh(inner)) * u
        act_buf[:, cols] = act.astype(BF16)
        a_buf[ch, 0:SUBLANES, :] = a_buf[ch, tm:tm + SUBLANES, :]
    y = jnp.dot(act_buf[...], wd_ref[...], preferred_element_type=F32)
    out_ref[...] = x + gate_ref[...] * _rms(y, gpost_ref[...], NORM_EPS)


def _layer_tail(attn, x, wo, gmix, gate_mix, gpre, scale, shift, gate, wg, wu, cw, cb, wd, gpost,
                tm=512, fc=FFN_CHUNK):
    bsz, seq, d = x.shape
    d_ff = wg.shape[1]
    n_chunks = d_ff // fc
    x_spec = pl.BlockSpec((None, tm, d), lambda b, i: (b, i, 0))
    row = pl.BlockSpec((None, 1, d), lambda b, i: (b, 0, 0))
    if isinstance(attn, tuple):
        outs, lses = attn
        n_groups = len(outs)
        attn_args = [*outs, *lses]
        attn_specs = [pl.BlockSpec((None, A_INNER // LANES, tm, LANES),
                                   lambda b, i: (b, 0, i, 0))] * (2 * n_groups)
    else:
        n_groups = 0
        attn_args = [attn]
        attn_specs = [x_spec]
    return pl.pallas_call(
        functools.partial(_tail_kernel, tm=tm, fc=fc, n_groups=n_groups),
        grid=(bsz, seq // tm),
        in_specs=attn_specs + [
            x_spec, _resident(wo.shape), _resident((1, d)), row,
            _resident((1, d)), row, row, row,
            _resident((d, d_ff)), _resident((d, d_ff)), _resident((CONV_WIDTH, d_ff)),
            _resident((1, d_ff)), _resident((d_ff, d)), _resident((1, d))],
        out_specs=x_spec,
        out_shape=jax.ShapeDtypeStruct((bsz, seq, d), F32),
        scratch_shapes=[pltpu.VMEM((n_chunks, tm + SUBLANES, fc), F32), pltpu.VMEM((tm, d_ff), BF16)],
        compiler_params=_params(2, sequential=True),
        name="layer_tail",
    )(*attn_args, x, wo, gmix.reshape(1, d), gate_mix, gpre.reshape(1, d), scale, shift, gate,
      wg, wu, cw, cb.reshape(1, d_ff), wd, gpost.reshape(1, d))


def _qkv1_kernel(x_ref, gq_ref, scq_ref, shq_ref, gkv_ref, sckv_ref, shkv_ref, wq_ref, wk_ref, wvt_ref,
                 c_ref, sa_ref, sb_ref, q_ref, k_ref, vt_ref):
    x = x_ref[...]
    rstd = lax.rsqrt(jnp.mean(x * x, axis=-1, keepdims=True) + NORM_EPS)
    xn = x * rstd
    hq = ((xn * gq_ref[...]) * (1.0 + scq_ref[...]) + shq_ref[...]).astype(BF16)
    hkv = ((xn * gkv_ref[...]) * (1.0 + sckv_ref[...]) + shkv_ref[...]).astype(BF16)
    c, sa, sb = c_ref[...], sa_ref[...], sb_ref[...]
    d = x.shape[1]
    for lo in range(0, d, PROJ_CHUNK):
        yq = jnp.dot(hq, wq_ref[:, lo:lo + PROJ_CHUNK], preferred_element_type=F32)
        yk = jnp.dot(hkv, wk_ref[:, lo:lo + PROJ_CHUNK], preferred_element_type=F32)
        for j in range(PROJ_CHUNK // LANES):
            cols = slice(lo + j * LANES, lo + (j + 1) * LANES)
            sub = slice(j * LANES, (j + 1) * LANES)
            q_ref[:, cols] = (_rope128(yq[:, sub], c, sa, sb) * (QK_SCALE * LOG2E)).astype(BF16)
            k_ref[:, cols] = _rope128(yk[:, sub], c, sa, sb).astype(BF16)
    for lo in range(0, d, VT_CHUNK):
        vt = lax.dot_general(wvt_ref[lo:lo + VT_CHUNK, :], hkv, (((1,), (1,)), ((), ())),
                             preferred_element_type=F32)
        vt_ref[lo:lo + VT_CHUNK, :] = vt.astype(BF16)


def _qkv1(x, gq, scq, shq, gkv, sckv, shkv, wq, wk, wvt, tables, tm):
    bsz, seq, d = x.shape
    x_spec = pl.BlockSpec((None, tm, d), lambda b, i: (b, i, 0))
    row = pl.BlockSpec((None, 1, d), lambda b, i: (b, 0, 0))
    tab = pl.BlockSpec((None, tm, LANES), lambda b, i: (b, i, 0))
    shape = jax.ShapeDtypeStruct((bsz, seq, d), BF16)
    return pl.pallas_call(
        _qkv1_kernel,
        grid=(bsz, seq // tm),
        in_specs=[x_spec, _resident((1, d)), row, row, _resident((1, d)), row, row,
                  _resident((d, d)), _resident((d, d)), _resident((d, d)), tab, tab, tab],
        out_specs=[x_spec, x_spec, pl.BlockSpec((None, None, d, tm), lambda b, i: (b, i, 0, 0))],
        out_shape=[shape, shape, jax.ShapeDtypeStruct((bsz, seq // tm, d, tm), BF16)],
        compiler_params=_params(2),
        name="qkv1",
    )(x, gq.reshape(1, d), scq, shq, gkv.reshape(1, d), sckv, shkv, wq, wk, wvt, *tables)


def _diff_kernel(lq1_ref, lk1_ref, lq2_ref, lk2_ref, g_ref, q_ref, k_ref, vt_ref, o_ref,
                 acc_ref, m_ref, l_ref, s_ref, bmax_ref, *, tq, lam_init):
    i = pl.program_id(2)
    lam = (jnp.exp(jnp.sum(lq1_ref[...] * lk1_ref[...], axis=-1, keepdims=True))
           - jnp.exp(jnp.sum(lq2_ref[...] * lk2_ref[...], axis=-1, keepdims=True)) + lam_init)
    q = q_ref[...]
    lane = lax.broadcasted_iota(jnp.int32, q.shape, 1)
    zero = jnp.zeros_like(q)
    q_maps = (jnp.where(lane < HEAD_DIM, q, zero), jnp.where(lane >= HEAD_DIM, q, zero))
    acc_ref[...] = jnp.zeros(acc_ref.shape, F32)
    m_ref[...] = jnp.full(m_ref.shape, NEG_INF, F32)
    l_ref[...] = jnp.zeros(l_ref.shape, F32)

    def scores(jb, slot, masked):
        start = jb * tq
        if not isinstance(jb, int):
            start = pl.multiple_of(start, tq)
        k = k_ref[pl.ds(start, tq), :]
        for c in range(2):
            s = lax.dot_general(k, q_maps[c], (((1,), (1,)), ((), ())), preferred_element_type=F32)
            if masked:
                key = lax.broadcasted_iota(jnp.int32, (tq, tq), 0)
                qry = lax.broadcasted_iota(jnp.int32, (tq, tq), 1)
                s = jnp.where(key <= qry, s, NEG_INF)
            s_ref[slot, c] = s
            bmax_ref[slot, c] = jnp.max(s, axis=0, keepdims=True)

    def update(jb, slot):
        vt = vt_ref[jb]
        for c in range(2):
            s = s_ref[slot, c]
            m_old = m_ref[c]
            m_new = jnp.maximum(m_old, bmax_ref[slot, c])
            alpha = jnp.exp2(m_old - m_new)
            p = jnp.exp2(s - m_new)
            l_ref[c] = alpha * l_ref[c] + jnp.sum(p, axis=0, keepdims=True)
            acc_ref[c] = alpha * acc_ref[c] + jnp.dot(vt, p.astype(BF16), preferred_element_type=F32)
            m_ref[c] = m_new

    def pair(jb, src, dst, masked):
        scores(jb + 1, dst, masked)
        update(jb, src)

    odd = (i & 1) == 1

    @pl.when(i == 0)
    def _():
        scores(0, 0, True)
        update(0, 0)

    @pl.when(i > 0)
    def _():
        scores(0, 0, False)

    def body(t, carry):
        pair(2 * t, 0, 1, False)
        pair(2 * t + 1, 1, 0, False)
        return carry

    lax.fori_loop(0, lax.shift_right_arithmetic(i - 1, 1), body, 0)

    @pl.when(odd)
    def _():
        pair(i - 1, 0, 1, True)
        update(i, 1)

    @pl.when(jnp.logical_and(i > 0, jnp.logical_not(odd)))
    def _():
        pair(i - 2, 0, 1, False)
        pair(i - 1, 1, 0, True)
        update(i, 0)

    o_t = acc_ref[0] * (1.0 / l_ref[0]) - lam * (acc_ref[1] * (1.0 / l_ref[1]))
    ms = jnp.mean(o_t * o_t, axis=0, keepdims=True)
    o_t = o_t * lax.rsqrt(ms + SUBLN_EPS) * g_ref[...] * (1.0 - lam_init)
    o_ref[...] = o_t.T.astype(BF16)


def _diff_attention(q, k, vt, lq1, lk1, lq2, lk2, subln, lam_init, tq):
    bsz, seq, d = q.shape
    width = 2 * HEAD_DIM
    vec = _resident((1, HEAD_DIM))
    k_spec = pl.BlockSpec((None, seq, width), lambda b, h, i: (b, 0, h))
    vt_spec = pl.BlockSpec((None, seq // tq, width, tq), lambda b, h, i: (b, 0, h, 0))
    qo_spec = pl.BlockSpec((None, tq, width), lambda b, h, i: (b, i, h))
    return pl.pallas_call(
        functools.partial(_diff_kernel, tq=tq, lam_init=lam_init),
        grid=(bsz, B_HEADS, seq // tq),
        in_specs=[vec, vec, vec, vec, _resident((width, 1)), qo_spec, k_spec, vt_spec],
        out_specs=qo_spec,
        out_shape=jax.ShapeDtypeStruct((bsz, seq, d), BF16),
        scratch_shapes=[pltpu.VMEM((2, width, tq), F32), pltpu.VMEM((2, 1, tq), F32),
                        pltpu.VMEM((2, 1, tq), F32), pltpu.VMEM((2, 2, tq, tq), F32),
                        pltpu.VMEM((2, 2, 1, tq), F32)],
        compiler_params=_params(3),
        name="diff_attn",
    )(lq1.reshape(1, HEAD_DIM), lk1.reshape(1, HEAD_DIM), lq2.reshape(1, HEAD_DIM),
      lk2.reshape(1, HEAD_DIM), subln.reshape(width, 1), q, k, vt)


def _split3(m):
    return jnp.split(m[:, None, :], 3, axis=-1)


def kernel(x, c, positions, mod_mix_w, mod_mix_b, mod_ffn_w, mod_ffn_b, norm_pre_mix, norm_post_mix, norm_pre_ffn, norm_post_ffn, ffn_w_gate, ffn_w_up, ffn_conv_w, ffn_conv_b, ffn_w_down, a_w_qkv, a_w_o, kv_norm, kv_mod_w, kv_mod_b, b_w_k, b_w_v, b_w_q, b_lambda_q1, b_lambda_k1, b_lambda_q2, b_lambda_k2, b_subln, b_w_o):
    mix_mod = _ada_mod(c, mod_mix_w, mod_mix_b)
    ffn_mod = _ada_mod(c, mod_ffn_w, mod_ffn_b)
    kv_mod = _ada_mod(c, kv_mod_w[None], kv_mod_b[None])[0]
    tables = _rope_tables(positions)

    def tail(attn, x, w_o, gate_mix, l):
        shift, scale, gate = _split3(ffn_mod[l])
        return _layer_tail(attn, x, w_o.astype(BF16), norm_post_mix[l], gate_mix,
                           norm_pre_ffn[l], scale, shift, gate, ffn_w_gate[l].astype(BF16),
                           ffn_w_up[l].astype(BF16), ffn_conv_w[l], ffn_conv_b[l],
                           ffn_w_down[l].astype(BF16), norm_post_ffn[l])

    shift, scale, gate = _split3(mix_mod[0])
    qkv_groups = _qkv0(x, norm_pre_mix[0], scale, shift, a_w_qkv[0].astype(BF16), tables)
    outs, lses = [], []
    for qkv, (_, dilation), (tq, unroll) in zip(qkv_groups, A_GROUPS, DIL_TILES):
        o, lse = _dilated_attention(qkv, dilation, tq, unroll)
        outs.append(o)
        lses.append(lse)
    x = tail((outs, lses), x, a_w_o[0], gate, 0)

    shift, scale, gate = _split3(mix_mod[1])
    kv_shift, kv_scale = jnp.split(kv_mod[:, None, :], 2, axis=-1)
    q, k, vt = _qkv1(x, norm_pre_mix[1], scale, shift, kv_norm, kv_scale, kv_shift,
                     b_w_q[0].astype(BF16), b_w_k.astype(BF16), b_w_v.T.astype(BF16), tables,
                     tm=DIFF_TILE)
    lam_init = 0.8 - 0.6 * math.exp(-0.3 * 1)
    o = _diff_attention(q, k, vt, b_lambda_q1[0], b_lambda_k1[0], b_lambda_q2[0], b_lambda_k2[0],
                        b_subln[0], lam_init, tq=DIFF_TILE)
    x = tail(o, x, b_w_o[0], gate, 1)
    return x
```

```python
import functools
import math

import jax
import jax.numpy as jnp
from jax import lax
from jax.experimental import pallas as pl
from jax.experimental.pallas import tpu as pltpu

D_MODEL = 1024
HEAD_DIM = 64
ROT_DIM = HEAD_DIM // 4
ROT_HALF = ROT_DIM // 2
ROPE_THETA = 500000.0
A_GROUPS = ((128, 1), (512, 4), (2048, 16))
A_HEADS = 8
A_INNER = A_HEADS * HEAD_DIM
B_HEADS = D_MODEL // (2 * HEAD_DIM)
CONV_WIDTH = 3
NORM_EPS = 1e-6
SUBLN_EPS = 1e-5
NEG_INF = -1e30
QK_SCALE = HEAD_DIM ** -0.5
LOG2E = math.log2(math.e)
GELU_C1 = math.sqrt(2.0 / math.pi)
GELU_C3 = GELU_C1 * 0.044715

LANES = 128
SUBLANES = 8
WIN = 128
PROJ_CHUNK = 512
VT_CHUNK = 256
FFN_CHUNK = 256
DIFF_TILE = 512
DIL_TILES = ((512, 1), (256, 2), (128, 4))
VMEM_LIMIT = 56 * 1024 * 1024

F32 = jnp.float32
BF16 = jnp.bfloat16


def _resident(shape):
    zeros = (0,) * len(shape)
    return pl.BlockSpec(shape, lambda *_: zeros, pipeline_mode=pl.Buffered(1))


def _params(n_axes, sequential=False):
    sem = ("arbitrary",) * n_axes if sequential else ("parallel",) * n_axes
    return pltpu.CompilerParams(dimension_semantics=sem, vmem_limit_bytes=VMEM_LIMIT)


def _rms(x, g, eps):
    ms = jnp.mean(x * x, axis=-1, keepdims=True)
    return x * lax.rsqrt(ms + eps) * g


def _rope128(y, c, sa, sb):
    return y * c + pltpu.roll(y, LANES - ROT_HALF, 1) * sa + pltpu.roll(y, ROT_HALF, 1) * sb


def _mod_kernel(c_ref, w_ref, b_ref, o_ref):
    c = c_ref[...]
    c_act = c / (1.0 + jnp.exp(-c))
    o_ref[...] = jnp.dot(c_act, w_ref[...], preferred_element_type=F32) + b_ref[...]


def _ada_mod(c, w, b, tn=1024):
    n_layers, d, n = w.shape
    bsz = c.shape[0]
    return pl.pallas_call(
        _mod_kernel,
        grid=(n_layers, n // tn),
        in_specs=[
            pl.BlockSpec((bsz, d), lambda l, j: (0, 0)),
            pl.BlockSpec((None, d, tn), lambda l, j: (l, 0, j)),
            pl.BlockSpec((None, 1, tn), lambda l, j: (l, 0, j)),
        ],
        out_specs=pl.BlockSpec((None, bsz, tn), lambda l, j: (l, 0, j)),
        out_shape=jax.ShapeDtypeStruct((n_layers, bsz, n), F32),
        compiler_params=_params(2),
        name="ada_mod",
    )(c, w, b.reshape(n_layers, 1, n))


def _rope_kernel(pos_ref, inv_ref, c_ref, sa_ref, sb_ref):
    ang = pos_ref[...].astype(F32) * inv_ref[...]
    cos = jnp.cos(ang)
    sin = jnp.sin(ang)
    lane = lax.broadcasted_iota(jnp.int32, ang.shape, 1) & (HEAD_DIM - 1)
    c_ref[...] = cos
    sa_ref[...] = jnp.where(lane < ROT_HALF, -sin, 0.0)
    sb_ref[...] = jnp.where((lane >= ROT_HALF) & (lane < ROT_DIM), sin, 0.0)


def _rope_tables(positions, ts=1024):
    bsz, seq = positions.shape
    inv = jnp.power(ROPE_THETA, -jnp.arange(0, ROT_DIM, 2, dtype=F32) / ROT_DIM)
    head = jnp.concatenate([inv, inv, jnp.zeros((HEAD_DIM - ROT_DIM,), F32)])
    inv_lane = jnp.tile(head, LANES // HEAD_DIM).reshape(1, LANES)
    spec = pl.BlockSpec((None, ts, LANES), lambda b, i: (b, i, 0))
    shape = jax.ShapeDtypeStruct((bsz, seq, LANES), F32)
    return pl.pallas_call(
        _rope_kernel,
        grid=(bsz, seq // ts),
        in_specs=[pl.BlockSpec((None, ts, 1), lambda b, i: (b, i, 0)),
                  pl.BlockSpec((1, LANES), lambda b, i: (0, 0))],
        out_specs=[spec, spec, spec],
        out_shape=[shape, shape, shape],
        compiler_params=_params(2),
        name="rope_tables",
    )(positions.reshape(bsz, seq, 1), inv_lane)


def _qkv0_kernel(x_ref, g_ref, sc_ref, sh_ref, w_ref, c_ref, sa_ref, sb_ref, *rest, tm):
    out_refs, stage_ref = rest[:-1], rest[-1]
    h = _rms(x_ref[...], g_ref[...], NORM_EPS) * (1.0 + sc_ref[...]) + sh_ref[...]
    hb = h.astype(BF16)
    c, sa, sb = c_ref[...], sa_ref[...], sb_ref[...]
    slabs = A_INNER // LANES
    stage = 0
    for grp, (_, dil) in enumerate(A_GROUPS):
        o_ref = out_refs[grp]
        for kind in range(3):
            lo = (grp * 3 + kind) * A_INNER
            y = jnp.dot(hb, w_ref[:, lo:lo + A_INNER], preferred_element_type=F32)
            for j in range(slabs):
                yj = y[:, j * LANES:(j + 1) * LANES]
                if kind < 2:
                    yj = _rope128(yj, c, sa, sb)
                if kind == 0:
                    yj = yj * QK_SCALE
                cols = slice(kind * A_INNER + j * LANES, kind * A_INNER + (j + 1) * LANES)
                if dil == 1:
                    o_ref[0, :, cols] = yj.astype(BF16)
                    continue
                stage_ref[stage] = yj
                for r in range(dil):
                    rows = stage_ref[stage, pl.ds(r, tm // dil, stride=dil), :]
                    o_ref[r, :, cols] = rows.astype(BF16)
                stage += 1


def _qkv0(x, gain, scale, shift, w, tables, tm=512):
    bsz, seq, d = x.shape
    n = w.shape[1]
    row = pl.BlockSpec((None, 1, d), lambda b, i: (b, 0, 0))
    tab = pl.BlockSpec((None, tm, LANES), lambda b, i: (b, i, 0))
    width = 3 * A_INNER
    dils = [dil for _, dil in A_GROUPS]
    n_stage = sum(3 * (A_INNER // LANES) for dil in dils if dil > 1)
    return pl.pallas_call(
        functools.partial(_qkv0_kernel, tm=tm),
        grid=(bsz, seq // tm),
        in_specs=[pl.BlockSpec((None, tm, d), lambda b, i: (b, i, 0)),
                  _resident((1, d)), row, row, _resident((d, n)), tab, tab, tab],
        out_specs=[pl.BlockSpec((None, dil, tm // dil, width), lambda b, i: (b, 0, i, 0)) for dil in dils],
        out_shape=[jax.ShapeDtypeStruct((bsz, dil, seq // dil, width), BF16) for dil in dils],
        scratch_shapes=[pltpu.VMEM((n_stage, tm, LANES), F32)],
        compiler_params=_params(2),
        name="qkv0",
    )(x, gain.reshape(1, d), scale, shift, w, *tables)


def _dil_kernel(q_ref, kp_ref, k_ref, vp_ref, v_ref, o_ref, lse_ref, *, tq, dil, unroll):
    first_tile = pl.program_id(1) == 0
    row = lax.broadcasted_iota(jnp.int32, (2 * WIN, 2 * WIN), 0) & (WIN - 1)
    col = lax.broadcasted_iota(jnp.int32, (2 * WIN, 2 * WIN), 1)
    band = (col >= row) & (col <= row + WIN)
    band_first = band & ((col >= WIN) | jnp.logical_not(first_tile))
    lane = lax.broadcasted_iota(jnp.int32, (WIN, LANES), 1)
    low_q = lane < HEAD_DIM
    low_kv = lax.broadcasted_iota(jnp.int32, (2 * WIN, LANES), 1) < HEAD_DIM

    def residue(r):
        for j in range(tq // WIN):
            rows = slice(j * WIN, (j + 1) * WIN)
            if dil == 1:
                out_rows = rows
            else:
                out_rows = pl.ds(j * WIN * dil + r, WIN, stride=dil)
            for hp in range(A_HEADS // 2):
                cols = slice(hp * LANES, (hp + 1) * LANES)
                q = q_ref[r, rows, cols]
                if j == 0:
                    kk = jnp.concatenate([kp_ref[r, :, cols], k_ref[r, 0:WIN, cols]], axis=0)
                    vv = jnp.concatenate([vp_ref[r, :, cols], v_ref[r, 0:WIN, cols]], axis=0)
                else:
                    kk = k_ref[r, (j - 1) * WIN:(j + 1) * WIN, cols]
                    vv = v_ref[r, (j - 1) * WIN:(j + 1) * WIN, cols]
                zq, zv = jnp.zeros_like(q), jnp.zeros_like(vv)
                q2 = jnp.concatenate([jnp.where(low_q, q, zq), jnp.where(low_q, zq, q)], axis=0)
                s = lax.dot_general(q2, kk, (((1,), (1,)), ((), ())), preferred_element_type=F32)
                s = jnp.where(band_first if j == 0 else band, s, NEG_INF)
                mx = jnp.max(s, axis=-1, keepdims=True)
                p = jnp.exp(s - mx)
                den = jnp.sum(p, axis=-1, keepdims=True)
                pb = p.astype(BF16)
                p2 = jnp.concatenate([pb[0:WIN], pb[WIN:2 * WIN]], axis=1)
                v2 = jnp.concatenate([jnp.where(low_kv, vv, zv), jnp.where(low_kv, zv, vv)], axis=0)
                o = jnp.dot(p2, v2, preferred_element_type=F32)
                inv = 1.0 / den
                o_ref[hp, out_rows, :] = o * jnp.where(low_q, inv[0:WIN], inv[WIN:2 * WIN])
                lse = mx + jnp.log(den)
                lse_ref[hp, out_rows, :] = jnp.where(low_q, lse[0:WIN], lse[WIN:2 * WIN])

    if dil == 1:
        residue(0)
    else:
        def body(t, carry):
            for u in range(unroll):
                residue(t * unroll + u)
            return carry
        lax.fori_loop(0, dil // unroll, body, 0)


def _dilated_attention(qkv, dilation, tq, unroll=1):
    bsz, _, length, width = qkv.shape
    seq = length * dilation
    sub = tq // WIN
    span = tq * dilation

    def cur(kind):
        return pl.BlockSpec((None, dilation, tq, A_INNER), lambda b, i: (b, 0, i, kind))

    def prev(kind):
        return pl.BlockSpec((None, dilation, WIN, A_INNER),
                            lambda b, i: (b, 0, jnp.maximum(i * sub - 1, 0), kind))

    return pl.pallas_call(
        functools.partial(_dil_kernel, tq=tq, dil=dilation, unroll=unroll),
        grid=(bsz, length // tq),
        in_specs=[cur(0), prev(1), cur(1), prev(2), cur(2)],
        out_specs=[pl.BlockSpec((None, A_INNER // LANES, span, LANES), lambda b, i: (b, 0, i, 0))] * 2,
        out_shape=[jax.ShapeDtypeStruct((bsz, A_INNER // LANES, seq, LANES), F32)] * 2,
        compiler_params=_params(2),
        name=f"dilated_attn_d{dilation}",
    )(qkv, qkv, qkv, qkv, qkv)


def _mixture(o_refs, l_refs):
    slabs = []
    for hp in range(A_HEADS // 2):
        lses = [l_ref[hp] for l_ref in l_refs]
        m = functools.reduce(jnp.maximum, lses)
        es = [jnp.exp(l - m) for l in lses]
        inv = 1.0 / functools.reduce(lambda a, b: a + b, es)
        mixed = functools.reduce(lambda a, b: a + b, [e * o_ref[hp] for e, o_ref in zip(es, o_refs)])
        slabs.append((mixed * inv).astype(BF16))
    return jnp.concatenate(slabs, axis=1)


def _tail_kernel(*refs, tm, fc, n_groups):
    n_attn = 2 * n_groups if n_groups else 1
    attn_refs = refs[:n_attn]
    (x_ref, wo_ref, gmix_ref, gatemix_ref, gpre_ref, sc_ref, sh_ref, gate_ref, wg_ref, wu_ref,
     cw_ref, cb_ref, wd_ref, gpost_ref, out_ref, a_buf, act_buf) = refs[n_attn:]
    n_chunks = a_buf.shape[0]

    @pl.when(pl.program_id(1) == 0)
    def _():
        for ch in range(n_chunks):
            a_buf[ch, 0:SUBLANES, :] = jnp.zeros((SUBLANES, fc), F32)

    if n_groups:
        attn = _mixture(attn_refs[:n_groups], attn_refs[n_groups:])
    else:
        attn = attn_refs[0][...]
    y_attn = jnp.dot(attn, wo_ref[...], preferred_element_type=F32)
    x = x_ref[...] + gatemix_ref[...] * _rms(y_attn, gmix_ref[...], NORM_EPS)
    h = _rms(x, gpre_ref[...], NORM_EPS) * (1.0 + sc_ref[...]) + sh_ref[...]
    hb = h.astype(BF16)

    def gate_up(ch):
        cols = slice(ch * fc, (ch + 1) * fc)
        a_buf[ch, SUBLANES:SUBLANES + tm, :] = jnp.dot(hb, wg_ref[:, cols], preferred_element_type=F32)
        return jnp.dot(hb, wu_ref[:, cols], preferred_element_type=F32)

    u_next = gate_up(0)
    for ch in range(n_chunks):
        cols = slice(ch * fc, (ch + 1) * fc)
        u = u_next
        if ch + 1 < n_chunks:
            u_next = gate_up(ch + 1)
        conv = cb_ref[:, cols]
        for t in range(CONV_WIDTH):
            start = SUBLANES - (CONV_WIDTH - 1) + t
            conv = conv + cw_ref[t:t + 1, cols] * a_buf[ch, start:start + tm, :]
        inner = conv * (GELU_C1 + GELU_C3 * (conv * conv))
        act = 0.5 * conv * (1.0 + jnp.tanh(inner)) * u
        act_buf[:, cols] = act.astype(BF16)
        a_buf[ch, 0:SUBLANES, :] = a_buf[ch, tm:tm + SUBLANES, :]
    y = jnp.dot(act_buf[...], wd_ref[...], preferred_element_type=F32)
    out_ref[...] = x + gate_ref[...] * _rms(y, gpost_ref[...], NORM_EPS)


def _layer_tail(attn, x, wo, gmix, gate_mix, gpre, scale, shift, gate, wg, wu, cw, cb, wd, gpost,
                tm=512, fc=FFN_CHUNK):
    bsz, seq, d = x.shape
    d_ff = wg.shape[1]
    n_chunks = d_ff // fc
    x_spec = pl.BlockSpec((None, tm, d), lambda b, i: (b, i, 0))
    row = pl.BlockSpec((None, 1, d), lambda b, i: (b, 0, 0))
    if isinstance(attn, tuple):
        outs, lses = attn
        n_groups = len(outs)
        attn_args = [*outs, *lses]
        attn_specs = [pl.BlockSpec((None, A_INNER // LANES, tm, LANES),
                                   lambda b, i: (b, 0, i, 0))] * (2 * n_groups)
    else:
        n_groups = 0
        attn_args = [attn]
        attn_specs = [x_spec]
    return pl.pallas_call(
        functools.partial(_tail_kernel, tm=tm, fc=fc, n_groups=n_groups),
        grid=(bsz, seq // tm),
        in_specs=attn_specs + [
            x_spec, _resident(wo.shape), _resident((1, d)), row,
            _resident((1, d)), row, row, row,
            _resident((d, d_ff)), _resident((d, d_ff)), _resident((CONV_WIDTH, d_ff)),
            _resident((1, d_ff)), _resident((d_ff, d)), _resident((1, d))],
        out_specs=x_spec,
        out_shape=jax.ShapeDtypeStruct((bsz, seq, d), F32),
        scratch_shapes=[pltpu.VMEM((n_chunks, tm + SUBLANES, fc), F32), pltpu.VMEM((tm, d_ff), BF16)],
        compiler_params=_params(2, sequential=True),
        name="layer_tail",
    )(*attn_args, x, wo, gmix.reshape(1, d), gate_mix, gpre.reshape(1, d), scale, shift, gate,
      wg, wu, cw, cb.reshape(1, d_ff), wd, gpost.reshape(1, d))


def _qkv1_kernel(x_ref, gq_ref, scq_ref, shq_ref, gkv_ref, sckv_ref, shkv_ref, wq_ref, wk_ref, wvt_ref,
                 c_ref, sa_ref, sb_ref, q_ref, k_ref, vt_ref):
    x = x_ref[...]
    rstd = lax.rsqrt(jnp.mean(x * x, axis=-1, keepdims=True) + NORM_EPS)
    xn = x * rstd
    hq = ((xn * gq_ref[...]) * (1.0 + scq_ref[...]) + shq_ref[...]).astype(BF16)
    hkv = ((xn * gkv_ref[...]) * (1.0 + sckv_ref[...]) + shkv_ref[...]).astype(BF16)
    c, sa, sb = c_ref[...], sa_ref[...], sb_ref[...]
    d = x.shape[1]
    for lo in range(0, d, PROJ_CHUNK):
        yq = jnp.dot(hq, wq_ref[:, lo:lo + PROJ_CHUNK], preferred_element_type=F32)
        yk = jnp.dot(hkv, wk_ref[:, lo:lo + PROJ_CHUNK], preferred_element_type=F32)
        for j in range(PROJ_CHUNK // LANES):
            cols = slice(lo + j * LANES, lo + (j + 1) * LANES)
            sub = slice(j * LANES, (j + 1) * LANES)
            q_ref[:, cols] = (_rope128(yq[:, sub], c, sa, sb) * (QK_SCALE * LOG2E)).astype(BF16)
            k_ref[:, cols] = _rope128(yk[:, sub], c, sa, sb).astype(BF16)
    for lo in range(0, d, VT_CHUNK):
        vt = lax.dot_general(wvt_ref[lo:lo + VT_CHUNK, :], hkv, (((1,), (1,)), ((), ())),
                             preferred_element_type=F32)
        vt_ref[lo:lo + VT_CHUNK, :] = vt.astype(BF16)


def _qkv1(x, gq, scq, shq, gkv, sckv, shkv, wq, wk, wvt, tables, tm):
    bsz, seq, d = x.shape
    x_spec = pl.BlockSpec((None, tm, d), lambda b, i: (b, i, 0))
    row = pl.BlockSpec((None, 1, d), lambda b, i: (b, 0, 0))
    tab = pl.BlockSpec((None, tm, LANES), lambda b, i: (b, i, 0))
    shape = jax.ShapeDtypeStruct((bsz, seq, d), BF16)
    return pl.pallas_call(
        _qkv1_kernel,
        grid=(bsz, seq // tm),
        in_specs=[x_spec, _resident((1, d)), row, row, _resident((1, d)), row, row,
                  _resident((d, d)), _resident((d, d)), _resident((d, d)), tab, tab, tab],
        out_specs=[x_spec, x_spec, pl.BlockSpec((None, None, d, tm), lambda b, i: (b, i, 0, 0))],
        out_shape=[shape, shape, jax.ShapeDtypeStruct((bsz, seq // tm, d, tm), BF16)],
        compiler_params=_params(2),
        name="qkv1",
    )(x, gq.reshape(1, d), scq, shq, gkv.reshape(1, d), sckv, shkv, wq, wk, wvt, *tables)


def _diff_kernel(lq1_ref, lk1_ref, lq2_ref, lk2_ref, g_ref, q_ref, k_ref, vt_ref, o_ref,
                 *scratch, tq, lam_init):
    lam = (jnp.exp(jnp.sum(lq1_ref[...] * lk1_ref[...], axis=-1, keepdims=True))
           - jnp.exp(jnp.sum(lq2_ref[...] * lk2_ref[...], axis=-1, keepdims=True)) + lam_init)
    scale = g_ref[...] * (1.0 - lam_init)

    def tile(i, carry):
        _diff_tile(i, lam, scale, q_ref, k_ref, vt_ref, o_ref, *scratch, tq=tq)
        return carry

    lax.fori_loop(0, q_ref.shape[0] // tq, tile, 0)


def _diff_tile(i, lam, scale, q_ref, k_ref, vt_ref, o_ref, acc_ref, m_ref, l_ref, s_ref, bmax_ref,
               *, tq):
    rows = pl.ds(pl.multiple_of(i * tq, tq), tq)
    q = q_ref[rows, :]
    lane = lax.broadcasted_iota(jnp.int32, q.shape, 1)
    zero = jnp.zeros_like(q)
    q_maps = (jnp.where(lane < HEAD_DIM, q, zero), jnp.where(lane >= HEAD_DIM, q, zero))
    acc_ref[...] = jnp.zeros(acc_ref.shape, F32)
    m_ref[...] = jnp.full(m_ref.shape, NEG_INF, F32)
    l_ref[...] = jnp.zeros(l_ref.shape, F32)

    half = tq // 2
    full = (0, tq, 0, tq)
    diag_a = (0, half, 0, tq)
    diag_b = (half, half, half, half)

    def scores(jb, slot, part, masked):
        k0, nk, q0, nq = part
        start = jb * tq + k0
        if not isinstance(jb, int):
            start = pl.multiple_of(start, half)
        k = k_ref[pl.ds(start, nk), :]
        for c in range(2):
            s = lax.dot_general(k, q_maps[c][q0:q0 + nq, :], (((1,), (1,)), ((), ())),
                                preferred_element_type=F32)
            if masked:
                key = lax.broadcasted_iota(jnp.int32, (nk, nq), 0) + k0
                qry = lax.broadcasted_iota(jnp.int32, (nk, nq), 1) + q0
                s = jnp.where(key <= qry, s, NEG_INF)
            s_ref[slot, c, 0:nk, q0:q0 + nq] = s
            bmax_ref[slot, c, :, q0:q0 + nq] = jnp.max(s, axis=0, keepdims=True)

    def update(jb, slot, part):
        k0, nk, q0, nq = part
        lanes = slice(q0, q0 + nq)
        vt = vt_ref[jb, :, k0:k0 + nk]
        for c in range(2):
            s = s_ref[slot, c, 0:nk, lanes]
            m_old = m_ref[c, :, lanes]
            m_new = jnp.maximum(m_old, bmax_ref[slot, c, :, lanes])
            alpha = jnp.exp2(m_old - m_new)
            p = jnp.exp2(s - m_new)
            l_ref[c, :, lanes] = alpha * l_ref[c, :, lanes] + jnp.sum(p, axis=0, keepdims=True)
            acc_ref[c, :, lanes] = (alpha * acc_ref[c, :, lanes]
                                    + jnp.dot(vt, p.astype(BF16), preferred_element_type=F32))
            m_ref[c, :, lanes] = m_new

    def diagonal(prev_slot):
        a_slot = 0 if prev_slot is None else 1 - prev_slot
        scores(i, a_slot, diag_a, True)
        if prev_slot is not None:
            update(i - 1, prev_slot, full)
        scores(i, 1 - a_slot, diag_b, True)
        update(i, a_slot, diag_a)
        update(i, 1 - a_slot, diag_b)

    odd = (i & 1) == 1

    @pl.when(i == 0)
    def _():
        diagonal(None)

    @pl.when(i > 0)
    def _():
        scores(0, 0, full, False)

    def body(t, carry):
        for slot in range(2):
            scores(2 * t + slot + 1, 1 - slot, full, False)
            update(2 * t + slot, slot, full)
        return carry

    lax.fori_loop(0, lax.shift_right_arithmetic(i - 1, 1), body, 0)

    @pl.when(odd)
    def _():
        diagonal(0)

    @pl.when(jnp.logical_and(i > 0, jnp.logical_not(odd)))
    def _():
        scores(i - 1, 1, full, False)
        update(i - 2, 0, full)
        diagonal(1)

    o_t = acc_ref[0] * (1.0 / l_ref[0]) - lam * (acc_ref[1] * (1.0 / l_ref[1]))
    ms = jnp.mean(o_t * o_t, axis=0, keepdims=True)
    o_t = o_t * lax.rsqrt(ms + SUBLN_EPS) * scale
    o_ref[rows, :] = o_t.T.astype(BF16)


def _diff_attention(q, k, vt, lq1, lk1, lq2, lk2, subln, lam_init, tq):
    bsz, seq, d = q.shape
    width = 2 * HEAD_DIM
    vec = _resident((1, HEAD_DIM))
    head_spec = pl.BlockSpec((None, seq, width), lambda b, h: (b, 0, h))
    vt_spec = pl.BlockSpec((None, seq // tq, width, tq), lambda b, h: (b, 0, h, 0))
    return pl.pallas_call(
        functools.partial(_diff_kernel, tq=tq, lam_init=lam_init),
        grid=(bsz, B_HEADS),
        in_specs=[vec, vec, vec, vec, _resident((width, 1)), head_spec, head_spec, vt_spec],
        out_specs=head_spec,
        out_shape=jax.ShapeDtypeStruct((bsz, seq, d), BF16),
        scratch_shapes=[pltpu.VMEM((2, width, tq), F32), pltpu.VMEM((2, 1, tq), F32),
                        pltpu.VMEM((2, 1, tq), F32), pltpu.VMEM((2, 2, tq, tq), F32),
                        pltpu.VMEM((2, 2, 1, tq), F32)],
        compiler_params=_params(2),
        name="diff_attn",
    )(lq1.reshape(1, HEAD_DIM), lk1.reshape(1, HEAD_DIM), lq2.reshape(1, HEAD_DIM),
      lk2.reshape(1, HEAD_DIM), subln.reshape(width, 1), q, k, vt)


def _split3(m):
    return jnp.split(m[:, None, :], 3, axis=-1)


def kernel(x, c, positions, mod_mix_w, mod_mix_b, mod_ffn_w, mod_ffn_b, norm_pre_mix, norm_post_mix, norm_pre_ffn, norm_post_ffn, ffn_w_gate, ffn_w_up, ffn_conv_w, ffn_conv_b, ffn_w_down, a_w_qkv, a_w_o, kv_norm, kv_mod_w, kv_mod_b, b_w_k, b_w_v, b_w_q, b_lambda_q1, b_lambda_k1, b_lambda_q2, b_lambda_k2, b_subln, b_w_o):
    mix_mod = _ada_mod(c, mod_mix_w, mod_mix_b)
    ffn_mod = _ada_mod(c, mod_ffn_w, mod_ffn_b)
    kv_mod = _ada_mod(c, kv_mod_w[None], kv_mod_b[None])[0]
    tables = _rope_tables(positions)

    def tail(attn, x, w_o, gate_mix, l):
        shift, scale, gate = _split3(ffn_mod[l])
        return _layer_tail(attn, x, w_o.astype(BF16), norm_post_mix[l], gate_mix,
                           norm_pre_ffn[l], scale, shift, gate, ffn_w_gate[l].astype(BF16),
                           ffn_w_up[l].astype(BF16), ffn_conv_w[l], ffn_conv_b[l],
                           ffn_w_down[l].astype(BF16), norm_post_ffn[l])

    shift, scale, gate = _split3(mix_mod[0])
    qkv_groups = _qkv0(x, norm_pre_mix[0], scale, shift, a_w_qkv[0].astype(BF16), tables)
    outs, lses = [], []
    for qkv, (_, dilation), (tq, unroll) in zip(qkv_groups, A_GROUPS, DIL_TILES):
        o, lse = _dilated_attention(qkv, dilation, tq, unroll)
        outs.append(o)
        lses.append(lse)
    x = tail((outs, lses), x, a_w_o[0], gate, 0)

    shift, scale, gate = _split3(mix_mod[1])
    kv_shift, kv_scale = jnp.split(kv_mod[:, None, :], 2, axis=-1)
    q, k, vt = _qkv1(x, norm_pre_mix[1], scale, shift, kv_norm, kv_scale, kv_shift,
                     b_w_q[0].astype(BF16), b_w_k.astype(BF16), b_w_v.T.astype(BF16), tables,
                     tm=DIFF_TILE)
    lam_init = 0.8 - 0.6 * math.exp(-0.3 * 1)
    o = _diff_attention(q, k, vt, b_lambda_q1[0], b_lambda_k1[0], b_lambda_q2[0], b_lambda_k2[0],
                        b_subln[0], lam_init, tq=DIFF_TILE)
    x = tail(o, x, b_w_o[0], gate, 1)
    return x
```

```python
import functools
import math

import jax
import jax.numpy as jnp
from jax import lax
from jax.experimental import pallas as pl
from jax.experimental.pallas import tpu as pltpu

D_MODEL = 1024
HEAD_DIM = 64
ROT_DIM = HEAD_DIM // 4
ROT_HALF = ROT_DIM // 2
ROPE_THETA = 500000.0
A_GROUPS = ((128, 1), (512, 4), (2048, 16))
A_HEADS = 8
A_INNER = A_HEADS * HEAD_DIM
B_HEADS = D_MODEL // (2 * HEAD_DIM)
CONV_WIDTH = 3
NORM_EPS = 1e-6
SUBLN_EPS = 1e-5
NEG_INF = -1e30
LOG2E = math.log2(math.e)
LN2 = math.log(2.0)
Q_SCALE = HEAD_DIM ** -0.5 * LOG2E
GELU_C1 = math.sqrt(2.0 / math.pi)
GELU_C3 = GELU_C1 * 0.044715

LANES = 128
SUBLANES = 8
WIN = 128
PROJ_CHUNK = 512
VT_CHUNK = 256
FFN_CHUNK = 256
DIFF_TILE = 512
ONES_ROWS = 16
DIL_TILES = ((512, 1), (256, 2), (128, 4))
VMEM_LIMIT = 56 * 1024 * 1024

F32 = jnp.float32
BF16 = jnp.bfloat16


def _resident(shape):
    zeros = (0,) * len(shape)
    return pl.BlockSpec(shape, lambda *_: zeros, pipeline_mode=pl.Buffered(1))


def _params(n_axes, sequential=False):
    sem = ("arbitrary",) * n_axes if sequential else ("parallel",) * n_axes
    return pltpu.CompilerParams(dimension_semantics=sem, vmem_limit_bytes=VMEM_LIMIT)


def _rms(x, g, eps):
    ms = jnp.mean(x * x, axis=-1, keepdims=True)
    return x * lax.rsqrt(ms + eps) * g


def _rope128(y, c, sa, sb):
    return y * c + pltpu.roll(y, LANES - ROT_HALF, 1) * sa + pltpu.roll(y, ROT_HALF, 1) * sb


def _mod_kernel(c_ref, w_ref, b_ref, o_ref):
    c = c_ref[...]
    c_act = c / (1.0 + jnp.exp(-c))
    o_ref[...] = jnp.dot(c_act, w_ref[...], preferred_element_type=F32) + b_ref[...]


def _ada_mod(c, w, b, tn=1024):
    n_layers, d, n = w.shape
    bsz = c.shape[0]
    return pl.pallas_call(
        _mod_kernel,
        grid=(n_layers, n // tn),
        in_specs=[
            pl.BlockSpec((bsz, d), lambda l, j: (0, 0)),
            pl.BlockSpec((None, d, tn), lambda l, j: (l, 0, j)),
            pl.BlockSpec((None, 1, tn), lambda l, j: (l, 0, j)),
        ],
        out_specs=pl.BlockSpec((None, bsz, tn), lambda l, j: (l, 0, j)),
        out_shape=jax.ShapeDtypeStruct((n_layers, bsz, n), F32),
        compiler_params=_params(2),
        name="ada_mod",
    )(c, w, b.reshape(n_layers, 1, n))


def _rope_kernel(pos_ref, inv_ref, c_ref, sa_ref, sb_ref):
    ang = pos_ref[...].astype(F32) * inv_ref[...]
    cos = jnp.cos(ang)
    sin = jnp.sin(ang)
    lane = lax.broadcasted_iota(jnp.int32, ang.shape, 1) & (HEAD_DIM - 1)
    c_ref[...] = cos
    sa_ref[...] = jnp.where(lane < ROT_HALF, -sin, 0.0)
    sb_ref[...] = jnp.where((lane >= ROT_HALF) & (lane < ROT_DIM), sin, 0.0)


def _rope_tables(positions, ts=1024):
    bsz, seq = positions.shape
    inv = jnp.power(ROPE_THETA, -jnp.arange(0, ROT_DIM, 2, dtype=F32) / ROT_DIM)
    head = jnp.concatenate([inv, inv, jnp.zeros((HEAD_DIM - ROT_DIM,), F32)])
    inv_lane = jnp.tile(head, LANES // HEAD_DIM).reshape(1, LANES)
    spec = pl.BlockSpec((None, ts, LANES), lambda b, i: (b, i, 0))
    shape = jax.ShapeDtypeStruct((bsz, seq, LANES), F32)
    return pl.pallas_call(
        _rope_kernel,
        grid=(bsz, seq // ts),
        in_specs=[pl.BlockSpec((None, ts, 1), lambda b, i: (b, i, 0)),
                  pl.BlockSpec((1, LANES), lambda b, i: (0, 0))],
        out_specs=[spec, spec, spec],
        out_shape=[shape, shape, shape],
        compiler_params=_params(2),
        name="rope_tables",
    )(positions.reshape(bsz, seq, 1), inv_lane)


def _qkv0_kernel(x_ref, g_ref, sc_ref, sh_ref, w_ref, c_ref, sa_ref, sb_ref, *rest, tm):
    out_refs, stage_ref = rest[:-1], rest[-1]
    h = _rms(x_ref[...], g_ref[...], NORM_EPS) * (1.0 + sc_ref[...]) + sh_ref[...]
    hb = h.astype(BF16)
    c, sa, sb = c_ref[...], sa_ref[...], sb_ref[...]
    slabs = A_INNER // LANES
    stage = 0
    for grp, (_, dil) in reversed(list(enumerate(A_GROUPS))):
        o_ref = out_refs[grp]
        for kind in range(3):
            lo = (grp * 3 + kind) * A_INNER
            y = jnp.dot(hb, w_ref[:, lo:lo + A_INNER], preferred_element_type=F32)
            for j in range(slabs):
                yj = y[:, j * LANES:(j + 1) * LANES]
                if kind < 2:
                    yj = _rope128(yj, c, sa, sb)
                if kind == 0:
                    yj = yj * Q_SCALE
                cols = slice(kind * A_INNER + j * LANES, kind * A_INNER + (j + 1) * LANES)
                if dil == 1:
                    o_ref[0, :, cols] = yj.astype(BF16)
                    continue
                stage_ref[stage] = yj
                for r in range(dil):
                    rows = stage_ref[stage, pl.ds(r, tm // dil, stride=dil), :]
                    o_ref[r, :, cols] = rows.astype(BF16)
                stage += 1


def _qkv0(x, gain, scale, shift, w, tables, tm=512):
    bsz, seq, d = x.shape
    n = w.shape[1]
    row = pl.BlockSpec((None, 1, d), lambda b, i: (b, 0, 0))
    tab = pl.BlockSpec((None, tm, LANES), lambda b, i: (b, i, 0))
    width = 3 * A_INNER
    dils = [dil for _, dil in A_GROUPS]
    n_stage = sum(3 * (A_INNER // LANES) for dil in dils if dil > 1)
    return pl.pallas_call(
        functools.partial(_qkv0_kernel, tm=tm),
        grid=(bsz, seq // tm),
        in_specs=[pl.BlockSpec((None, tm, d), lambda b, i: (b, i, 0)),
                  _resident((1, d)), row, row, _resident((d, n)), tab, tab, tab],
        out_specs=[pl.BlockSpec((None, dil, tm // dil, width), lambda b, i: (b, 0, i, 0)) for dil in dils],
        out_shape=[jax.ShapeDtypeStruct((bsz, dil, seq // dil, width), BF16) for dil in dils],
        scratch_shapes=[pltpu.VMEM((n_stage, tm, LANES), F32)],
        compiler_params=_params(2),
        name="qkv0",
    )(x, gain.reshape(1, d), scale, shift, w, *tables)


def _dil_kernel(q_ref, kp_ref, k_ref, vp_ref, v_ref, o_ref, lse_ref, *, tq, dil, unroll):
    first_tile = pl.program_id(1) == 0
    row = lax.broadcasted_iota(jnp.int32, (2 * WIN, 2 * WIN), 0) & (WIN - 1)
    col = lax.broadcasted_iota(jnp.int32, (2 * WIN, 2 * WIN), 1)
    band = (col >= row) & (col <= row + WIN)
    band_first = band & ((col >= WIN) | jnp.logical_not(first_tile))
    lane = lax.broadcasted_iota(jnp.int32, (WIN, LANES), 1)
    low_q = lane < HEAD_DIM
    low_kv = lax.broadcasted_iota(jnp.int32, (2 * WIN, LANES), 1) < HEAD_DIM

    def residue(r):
        for j in range(tq // WIN):
            rows = slice(j * WIN, (j + 1) * WIN)
            if dil == 1:
                out_rows = rows
            else:
                out_rows = pl.ds(j * WIN * dil + r, WIN, stride=dil)
            for hp in range(A_HEADS // 2):
                cols = slice(hp * LANES, (hp + 1) * LANES)
                q = q_ref[r, rows, cols]
                if j == 0:
                    kk = jnp.concatenate([kp_ref[r, :, cols], k_ref[r, 0:WIN, cols]], axis=0)
                    vv = jnp.concatenate([vp_ref[r, :, cols], v_ref[r, 0:WIN, cols]], axis=0)
                else:
                    kk = k_ref[r, (j - 1) * WIN:(j + 1) * WIN, cols]
                    vv = v_ref[r, (j - 1) * WIN:(j + 1) * WIN, cols]
                zq, zv = jnp.zeros_like(q), jnp.zeros_like(vv)
                q2 = jnp.concatenate([jnp.where(low_q, q, zq), jnp.where(low_q, zq, q)], axis=0)
                s = lax.dot_general(q2, kk, (((1,), (1,)), ((), ())), preferred_element_type=F32)
                s = jnp.where(band_first if j == 0 else band, s, NEG_INF)
                mx = jnp.max(s, axis=-1, keepdims=True)
                p = jnp.exp2(s - mx)
                den = jnp.sum(p, axis=-1, keepdims=True)
                pb = p.astype(BF16)
                p2 = jnp.concatenate([pb[0:WIN], pb[WIN:2 * WIN]], axis=1)
                v2 = jnp.concatenate([jnp.where(low_kv, vv, zv), jnp.where(low_kv, zv, vv)], axis=0)
                o = jnp.dot(p2, v2, preferred_element_type=F32)
                inv = 1.0 / den
                o_ref[hp, out_rows, :] = o * jnp.where(low_q, inv[0:WIN], inv[WIN:2 * WIN])
                lse = mx * LN2 + jnp.log(den)
                lse_ref[hp, out_rows, :] = jnp.where(low_q, lse[0:WIN], lse[WIN:2 * WIN])

    if dil == 1:
        residue(0)
    else:
        def body(t, carry):
            for u in range(unroll):
                residue(t * unroll + u)
            return carry
        lax.fori_loop(0, dil // unroll, body, 0)


def _dilated_attention(qkv, dilation, tq, unroll=1):
    bsz, _, length, width = qkv.shape
    seq = length * dilation
    sub = tq // WIN
    span = tq * dilation

    def cur(kind):
        return pl.BlockSpec((None, dilation, tq, A_INNER), lambda b, i: (b, 0, i, kind))

    def prev(kind):
        return pl.BlockSpec((None, dilation, WIN, A_INNER),
                            lambda b, i: (b, 0, jnp.maximum(i * sub - 1, 0), kind))

    return pl.pallas_call(
        functools.partial(_dil_kernel, tq=tq, dil=dilation, unroll=unroll),
        grid=(bsz, length // tq),
        in_specs=[cur(0), prev(1), cur(1), prev(2), cur(2)],
        out_specs=[pl.BlockSpec((None, A_INNER // LANES, span, LANES), lambda b, i: (b, 0, i, 0))] * 2,
        out_shape=[jax.ShapeDtypeStruct((bsz, A_INNER // LANES, seq, LANES), F32)] * 2,
        compiler_params=_params(2),
        name=f"dilated_attn_d{dilation}",
    )(qkv, qkv, qkv, qkv, qkv)


def _mixture(o_refs, l_refs):
    slabs = []
    for hp in range(A_HEADS // 2):
        lses = [l_ref[hp] for l_ref in l_refs]
        m = functools.reduce(jnp.maximum, lses)
        es = [jnp.exp(l - m) for l in lses]
        inv = 1.0 / functools.reduce(lambda a, b: a + b, es)
        mixed = functools.reduce(lambda a, b: a + b, [e * o_ref[hp] for e, o_ref in zip(es, o_refs)])
        slabs.append((mixed * inv).astype(BF16))
    return jnp.concatenate(slabs, axis=1)


def _tail_kernel(*refs, tm, fc, n_groups):
    n_attn = 2 * n_groups if n_groups else 1
    attn_refs = refs[:n_attn]
    (x_ref, wo_ref, gmix_ref, gatemix_ref, gpre_ref, sc_ref, sh_ref, gate_ref, wg_ref, wu_ref,
     cw_ref, cb_ref, wd_ref, gpost_ref, out_ref, a_buf, act_buf) = refs[n_attn:]
    n_chunks = a_buf.shape[0]

    @pl.when(pl.program_id(1) == 0)
    def _():
        for ch in range(n_chunks):
            a_buf[ch, 0:SUBLANES, :] = jnp.zeros((SUBLANES, fc), F32)

    if n_groups:
        attn = _mixture(attn_refs[:n_groups], attn_refs[n_groups:])
    else:
        attn = attn_refs[0][...]
    y_attn = jnp.dot(attn, wo_ref[...], preferred_element_type=F32)
    x = x_ref[...] + gatemix_ref[...] * _rms(y_attn, gmix_ref[...], NORM_EPS)
    h = _rms(x, gpre_ref[...], NORM_EPS) * (1.0 + sc_ref[...]) + sh_ref[...]
    hb = h.astype(BF16)

    def gate_up(ch):
        cols = slice(ch * fc, (ch + 1) * fc)
        a_buf[ch, SUBLANES:SUBLANES + tm, :] = jnp.dot(hb, wg_ref[:, cols], preferred_element_type=F32)
        return jnp.dot(hb, wu_ref[:, cols], preferred_element_type=F32)

    u_next = gate_up(0)
    for ch in range(n_chunks):
        cols = slice(ch * fc, (ch + 1) * fc)
        u = u_next
        if ch + 1 < n_chunks:
            u_next = gate_up(ch + 1)
        conv = cb_ref[:, cols]
        for t in range(CONV_WIDTH):
            start = SUBLANES - (CONV_WIDTH - 1) + t
            conv = conv + cw_ref[t:t + 1, cols] * a_buf[ch, start:start + tm, :]
        inner = conv * (GELU_C1 + GELU_C3 * (conv * conv))
        act = 0.5 * conv * (1.0 + jnp.tanh(inner)) * u
        act_buf[:, cols] = act.astype(BF16)
        a_buf[ch, 0:SUBLANES, :] = a_buf[ch, tm:tm + SUBLANES, :]
    y = jnp.dot(act_buf[...], wd_ref[...], preferred_element_type=F32)
    out_ref[...] = x + gate_ref[...] * _rms(y, gpost_ref[...], NORM_EPS)


def _layer_tail(attn, x, wo, gmix, gate_mix, gpre, scale, shift, gate, wg, wu, cw, cb, wd, gpost,
                tm=512, fc=FFN_CHUNK):
    bsz, seq, d = x.shape
    d_ff = wg.shape[1]
    n_chunks = d_ff // fc
    x_spec = pl.BlockSpec((None, tm, d), lambda b, i: (b, i, 0))
    row = pl.BlockSpec((None, 1, d), lambda b, i: (b, 0, 0))
    if isinstance(attn, tuple):
        outs, lses = attn
        n_groups = len(outs)
        attn_args = [*outs, *lses]
        attn_specs = [pl.BlockSpec((None, A_INNER // LANES, tm, LANES),
                                   lambda b, i: (b, 0, i, 0))] * (2 * n_groups)
    else:
        n_groups = 0
        attn_args = [attn]
        attn_specs = [x_spec]
    return pl.pallas_call(
        functools.partial(_tail_kernel, tm=tm, fc=fc, n_groups=n_groups),
        grid=(bsz, seq // tm),
        in_specs=attn_specs + [
            x_spec, _resident(wo.shape), _resident((1, d)), row,
            _resident((1, d)), row, row, row,
            _resident((d, d_ff)), _resident((d, d_ff)), _resident((CONV_WIDTH, d_ff)),
            _resident((1, d_ff)), _resident((d_ff, d)), _resident((1, d))],
        out_specs=x_spec,
        out_shape=jax.ShapeDtypeStruct((bsz, seq, d), F32),
        scratch_shapes=[pltpu.VMEM((n_chunks, tm + SUBLANES, fc), F32), pltpu.VMEM((tm, d_ff), BF16)],
        compiler_params=_params(2, sequential=True),
        name="layer_tail",
    )(*attn_args, x, wo, gmix.reshape(1, d), gate_mix, gpre.reshape(1, d), scale, shift, gate,
      wg, wu, cw, cb.reshape(1, d_ff), wd, gpost.reshape(1, d))


def _qkv1_kernel(x_ref, gq_ref, scq_ref, shq_ref, gkv_ref, sckv_ref, shkv_ref, wq_ref, wk_ref, wvt_ref,
                 c_ref, sa_ref, sb_ref, q_ref, k_ref, vt_ref):
    x = x_ref[...]
    rstd = lax.rsqrt(jnp.mean(x * x, axis=-1, keepdims=True) + NORM_EPS)
    xn = x * rstd
    hq = ((xn * gq_ref[...]) * (1.0 + scq_ref[...]) + shq_ref[...]).astype(BF16)
    hkv = ((xn * gkv_ref[...]) * (1.0 + sckv_ref[...]) + shkv_ref[...]).astype(BF16)
    c, sa, sb = c_ref[...], sa_ref[...], sb_ref[...]
    d = x.shape[1]
    for lo in range(0, d, PROJ_CHUNK):
        yq = jnp.dot(hq, wq_ref[:, lo:lo + PROJ_CHUNK], preferred_element_type=F32)
        yk = jnp.dot(hkv, wk_ref[:, lo:lo + PROJ_CHUNK], preferred_element_type=F32)
        for j in range(PROJ_CHUNK // LANES):
            cols = slice(lo + j * LANES, lo + (j + 1) * LANES)
            sub = slice(j * LANES, (j + 1) * LANES)
            q_ref[:, cols] = (_rope128(yq[:, sub], c, sa, sb) * Q_SCALE).astype(BF16)
            k_ref[:, cols] = _rope128(yk[:, sub], c, sa, sb).astype(BF16)
    for lo in range(0, d, VT_CHUNK):
        vt = lax.dot_general(wvt_ref[lo:lo + VT_CHUNK, :], hkv, (((1,), (1,)), ((), ())),
                             preferred_element_type=F32)
        vt_ref[lo:lo + VT_CHUNK, :] = vt.astype(BF16)


def _qkv1(x, gq, scq, shq, gkv, sckv, shkv, wq, wk, wvt, tables, tm):
    bsz, seq, d = x.shape
    x_spec = pl.BlockSpec((None, tm, d), lambda b, i: (b, i, 0))
    row = pl.BlockSpec((None, 1, d), lambda b, i: (b, 0, 0))
    tab = pl.BlockSpec((None, tm, LANES), lambda b, i: (b, i, 0))
    shape = jax.ShapeDtypeStruct((bsz, seq, d), BF16)
    return pl.pallas_call(
        _qkv1_kernel,
        grid=(bsz, seq // tm),
        in_specs=[x_spec, _resident((1, d)), row, row, _resident((1, d)), row, row,
                  _resident((d, d)), _resident((d, d)), _resident((d, d)), tab, tab, tab],
        out_specs=[x_spec, x_spec, pl.BlockSpec((None, None, d, tm), lambda b, i: (b, i, 0, 0))],
        out_shape=[shape, shape, jax.ShapeDtypeStruct((bsz, seq // tm, d, tm), BF16)],
        compiler_params=_params(2),
        name="qkv1",
    )(x, gq.reshape(1, d), scq, shq, gkv.reshape(1, d), sckv, shkv, wq, wk, wvt, *tables)


def _diff_kernel(lq1_ref, lk1_ref, lq2_ref, lk2_ref, g_ref, q_ref, k_ref, vt_ref, o_ref,
                 *scratch, tq, lam_init):
    lam = (jnp.exp(jnp.sum(lq1_ref[...] * lk1_ref[...], axis=-1, keepdims=True))
           - jnp.exp(jnp.sum(lq2_ref[...] * lk2_ref[...], axis=-1, keepdims=True)) + lam_init)
    scale = g_ref[...] * (1.0 - lam_init)

    def tile(i, carry):
        _diff_tile(i, lam, scale, q_ref, k_ref, vt_ref, o_ref, *scratch, tq=tq)
        return carry

    lax.fori_loop(0, q_ref.shape[0] // tq, tile, 0)


def _diff_tile(i, lam, scale, q_ref, k_ref, vt_ref, o_ref, acc_ref, m_ref, s_ref, bmax_ref, *, tq):
    rows = pl.ds(pl.multiple_of(i * tq, tq), tq)
    q = q_ref[rows, :]
    lane = lax.broadcasted_iota(jnp.int32, q.shape, 1)
    zero = jnp.zeros_like(q)
    q_maps = (jnp.where(lane < HEAD_DIM, q, zero), jnp.where(lane >= HEAD_DIM, q, zero))
    acc_ref[...] = jnp.zeros(acc_ref.shape, F32)
    m_ref[...] = jnp.full(m_ref.shape, NEG_INF, F32)

    half = tq // 2
    full = (0, tq, 0, tq)
    diag_a = (0, half, 0, tq)
    diag_b = (half, half, half, half)

    def scores(jb, slot, part, masked):
        k0, nk, q0, nq = part
        start = jb * tq + k0
        if not isinstance(jb, int):
            start = pl.multiple_of(start, half)
        k = k_ref[pl.ds(start, nk), :]
        for c in range(2):
            s = lax.dot_general(k, q_maps[c][q0:q0 + nq, :], (((1,), (1,)), ((), ())),
                                preferred_element_type=F32)
            if masked:
                key = lax.broadcasted_iota(jnp.int32, (nk, nq), 0) + k0
                qry = lax.broadcasted_iota(jnp.int32, (nk, nq), 1) + q0
                s = jnp.where(key <= qry, s, NEG_INF)
            s_ref[slot, c, 0:nk, q0:q0 + nq] = s
            bmax_ref[slot, c, :, q0:q0 + nq] = jnp.max(s, axis=0, keepdims=True)

    def update(jb, slot, part):
        k0, nk, q0, nq = part
        lanes = slice(q0, q0 + nq)
        vt = jnp.concatenate([vt_ref[jb, :, k0:k0 + nk], jnp.ones((ONES_ROWS, nk), BF16)], axis=0)
        for c in range(2):
            s = s_ref[slot, c, 0:nk, lanes]
            m_old = m_ref[c, :, lanes]
            m_new = jnp.maximum(m_old, bmax_ref[slot, c, :, lanes])
            alpha = jnp.exp2(m_old - m_new)
            p = jnp.exp2(s - m_new)
            acc_ref[c, :, lanes] = (alpha * acc_ref[c, :, lanes]
                                    + jnp.dot(vt, p.astype(BF16), preferred_element_type=F32))
            m_ref[c, :, lanes] = m_new

    def diagonal(prev_slot):
        a_slot = 0 if prev_slot is None else 1 - prev_slot
        scores(i, a_slot, diag_a, True)
        if prev_slot is not None:
            update(i - 1, prev_slot, full)
        scores(i, 1 - a_slot, diag_b, True)
        update(i, a_slot, diag_a)
        update(i, 1 - a_slot, diag_b)

    odd = (i & 1) == 1

    @pl.when(i == 0)
    def _():
        diagonal(None)

    @pl.when(i > 0)
    def _():
        scores(0, 0, full, False)

    def body(t, carry):
        for slot in range(2):
            scores(2 * t + slot + 1, 1 - slot, full, False)
            update(2 * t + slot, slot, full)
        return carry

    lax.fori_loop(0, lax.shift_right_arithmetic(i - 1, 1), body, 0)

    @pl.when(odd)
    def _():
        diagonal(0)

    @pl.when(jnp.logical_and(i > 0, jnp.logical_not(odd)))
    def _():
        scores(i - 1, 1, full, False)
        update(i - 2, 0, full)
        diagonal(1)

    width = 2 * HEAD_DIM
    outs = [acc_ref[c, 0:width, :] * (1.0 / acc_ref[c, width:width + 1, :]) for c in range(2)]
    o_t = outs[0] - lam * outs[1]
    ms = jnp.mean(o_t * o_t, axis=0, keepdims=True)
    o_t = o_t * lax.rsqrt(ms + SUBLN_EPS) * scale
    o_ref[rows, :] = o_t.T.astype(BF16)


def _diff_attention(q, k, vt, lq1, lk1, lq2, lk2, subln, lam_init, tq):
    bsz, seq, d = q.shape
    width = 2 * HEAD_DIM
    vec = _resident((1, HEAD_DIM))
    head_spec = pl.BlockSpec((None, seq, width), lambda b, h: (b, 0, h))
    vt_spec = pl.BlockSpec((None, seq // tq, width, tq), lambda b, h: (b, 0, h, 0))
    return pl.pallas_call(
        functools.partial(_diff_kernel, tq=tq, lam_init=lam_init),
        grid=(bsz, B_HEADS),
        in_specs=[vec, vec, vec, vec, _resident((width, 1)), head_spec, head_spec, vt_spec],
        out_specs=head_spec,
        out_shape=jax.ShapeDtypeStruct((bsz, seq, d), BF16),
        scratch_shapes=[pltpu.VMEM((2, width + ONES_ROWS, tq), F32), pltpu.VMEM((2, 1, tq), F32),
                        pltpu.VMEM((2, 2, tq, tq), F32),
                        pltpu.VMEM((2, 2, 1, tq), F32)],
        compiler_params=_params(2),
        name="diff_attn",
    )(lq1.reshape(1, HEAD_DIM), lk1.reshape(1, HEAD_DIM), lq2.reshape(1, HEAD_DIM),
      lk2.reshape(1, HEAD_DIM), subln.reshape(width, 1), q, k, vt)


def _split3(m):
    return jnp.split(m[:, None, :], 3, axis=-1)


def kernel(x, c, positions, mod_mix_w, mod_mix_b, mod_ffn_w, mod_ffn_b, norm_pre_mix, norm_post_mix, norm_pre_ffn, norm_post_ffn, ffn_w_gate, ffn_w_up, ffn_conv_w, ffn_conv_b, ffn_w_down, a_w_qkv, a_w_o, kv_norm, kv_mod_w, kv_mod_b, b_w_k, b_w_v, b_w_q, b_lambda_q1, b_lambda_k1, b_lambda_q2, b_lambda_k2, b_subln, b_w_o):
    mix_mod = _ada_mod(c, mod_mix_w, mod_mix_b)
    ffn_mod = _ada_mod(c, mod_ffn_w, mod_ffn_b)
    kv_mod = _ada_mod(c, kv_mod_w[None], kv_mod_b[None])[0]
    tables = _rope_tables(positions)

    def tail(attn, x, w_o, gate_mix, l):
        shift, scale, gate = _split3(ffn_mod[l])
        return _layer_tail(attn, x, w_o.astype(BF16), norm_post_mix[l], gate_mix,
                           norm_pre_ffn[l], scale, shift, gate, ffn_w_gate[l].astype(BF16),
                           ffn_w_up[l].astype(BF16), ffn_conv_w[l], ffn_conv_b[l],
                           ffn_w_down[l].astype(BF16), norm_post_ffn[l])

    shift, scale, gate = _split3(mix_mod[0])
    qkv_groups = _qkv0(x, norm_pre_mix[0], scale, shift, a_w_qkv[0].astype(BF16), tables)
    outs, lses = [], []
    for qkv, (_, dilation), (tq, unroll) in zip(qkv_groups, A_GROUPS, DIL_TILES):
        o, lse = _dilated_attention(qkv, dilation, tq, unroll)
        outs.append(o)
        lses.append(lse)
    x = tail((outs, lses), x, a_w_o[0], gate, 0)

    shift, scale, gate = _split3(mix_mod[1])
    kv_shift, kv_scale = jnp.split(kv_mod[:, None, :], 2, axis=-1)
    q, k, vt = _qkv1(x, norm_pre_mix[1], scale, shift, kv_norm, kv_scale, kv_shift,
                     b_w_q[0].astype(BF16), b_w_k.astype(BF16), b_w_v.T.astype(BF16), tables,
                     tm=DIFF_TILE)
    lam_init = 0.8 - 0.6 * math.exp(-0.3 * 1)
    o = _diff_attention(q, k, vt, b_lambda_q1[0], b_lambda_k1[0], b_lambda_q2[0], b_lambda_k2[0],
                        b_subln[0], lam_init, tq=DIFF_TILE)
    x = tail(o, x, b_w_o[0], gate, 1)
    return x
```

```python
import functools
import math

import jax
import jax.numpy as jnp
from jax import lax
from jax.experimental import pallas as pl
from jax.experimental.pallas import tpu as pltpu

D_MODEL = 1024
HEAD_DIM = 64
ROT_DIM = HEAD_DIM // 4
ROT_HALF = ROT_DIM // 2
ROPE_THETA = 500000.0
A_GROUPS = ((128, 1), (512, 4), (2048, 16))
A_HEADS = 8
A_INNER = A_HEADS * HEAD_DIM
B_HEADS = D_MODEL // (2 * HEAD_DIM)
CONV_WIDTH = 3
NORM_EPS = 1e-6
SUBLN_EPS = 1e-5
NEG_INF = -1e30
LOG2E = math.log2(math.e)
LN2 = math.log(2.0)
Q_SCALE = HEAD_DIM ** -0.5 * LOG2E
GELU_C1 = math.sqrt(2.0 / math.pi)
GELU_C3 = GELU_C1 * 0.044715

LANES = 128
SUBLANES = 8
WIN = 128
PROJ_CHUNK = 512
VT_CHUNK = 256
FFN_CHUNK = 256
DIFF_TILE = 512
ONES_ROWS = 16
DIL_TILES = ((512, 1), (256, 2), (128, 4))
VMEM_LIMIT = 56 * 1024 * 1024

F32 = jnp.float32
BF16 = jnp.bfloat16


def _resident(shape):
    zeros = (0,) * len(shape)
    return pl.BlockSpec(shape, lambda *_: zeros, pipeline_mode=pl.Buffered(1))


def _params(n_axes, sequential=False):
    sem = ("arbitrary",) * n_axes if sequential else ("parallel",) * n_axes
    return pltpu.CompilerParams(dimension_semantics=sem, vmem_limit_bytes=VMEM_LIMIT)


def _rms(x, g, eps):
    ms = jnp.mean(x * x, axis=-1, keepdims=True)
    return x * lax.rsqrt(ms + eps) * g


def _rope128(y, c, sa, sb):
    return y * c + pltpu.roll(y, LANES - ROT_HALF, 1) * sa + pltpu.roll(y, ROT_HALF, 1) * sb


def _mod_kernel(c_ref, w_ref, b_ref, o_ref):
    c = c_ref[...]
    c_act = c / (1.0 + jnp.exp(-c))
    o_ref[...] = jnp.dot(c_act, w_ref[...], preferred_element_type=F32) + b_ref[...]


def _ada_mod(c, w, b, tn=1024):
    n_layers, d, n = w.shape
    bsz = c.shape[0]
    return pl.pallas_call(
        _mod_kernel,
        grid=(n_layers, n // tn),
        in_specs=[
            pl.BlockSpec((bsz, d), lambda l, j: (0, 0)),
            pl.BlockSpec((None, d, tn), lambda l, j: (l, 0, j)),
            pl.BlockSpec((None, 1, tn), lambda l, j: (l, 0, j)),
        ],
        out_specs=pl.BlockSpec((None, bsz, tn), lambda l, j: (l, 0, j)),
        out_shape=jax.ShapeDtypeStruct((n_layers, bsz, n), F32),
        compiler_params=_params(2),
        name="ada_mod",
    )(c, w, b.reshape(n_layers, 1, n))


def _rope_kernel(pos_ref, inv_ref, c_ref, sa_ref, sb_ref):
    ang = pos_ref[...].astype(F32) * inv_ref[...]
    cos = jnp.cos(ang)
    sin = jnp.sin(ang)
    lane = lax.broadcasted_iota(jnp.int32, ang.shape, 1) & (HEAD_DIM - 1)
    c_ref[...] = cos
    sa_ref[...] = jnp.where(lane < ROT_HALF, -sin, 0.0)
    sb_ref[...] = jnp.where((lane >= ROT_HALF) & (lane < ROT_DIM), sin, 0.0)


def _rope_tables(positions, ts=1024):
    bsz, seq = positions.shape
    inv = jnp.power(ROPE_THETA, -jnp.arange(0, ROT_DIM, 2, dtype=F32) / ROT_DIM)
    head = jnp.concatenate([inv, inv, jnp.zeros((HEAD_DIM - ROT_DIM,), F32)])
    inv_lane = jnp.tile(head, LANES // HEAD_DIM).reshape(1, LANES)
    spec = pl.BlockSpec((None, ts, LANES), lambda b, i: (b, i, 0))
    shape = jax.ShapeDtypeStruct((bsz, seq, LANES), F32)
    return pl.pallas_call(
        _rope_kernel,
        grid=(bsz, seq // ts),
        in_specs=[pl.BlockSpec((None, ts, 1), lambda b, i: (b, i, 0)),
                  pl.BlockSpec((1, LANES), lambda b, i: (0, 0))],
        out_specs=[spec, spec, spec],
        out_shape=[shape, shape, shape],
        compiler_params=_params(2),
        name="rope_tables",
    )(positions.reshape(bsz, seq, 1), inv_lane)


def _qkv0_kernel(x_ref, g_ref, sc_ref, sh_ref, w_ref, c_ref, sa_ref, sb_ref, *rest, tm):
    out_refs, stage_ref = rest[:-1], rest[-1]
    h = _rms(x_ref[...], g_ref[...], NORM_EPS) * (1.0 + sc_ref[...]) + sh_ref[...]
    hb = h.astype(BF16)
    c, sa, sb = c_ref[...], sa_ref[...], sb_ref[...]
    slabs = A_INNER // LANES
    stage = 0
    for grp, (_, dil) in reversed(list(enumerate(A_GROUPS))):
        o_ref = out_refs[grp]
        for kind in range(3):
            lo = (grp * 3 + kind) * A_INNER
            y = jnp.dot(hb, w_ref[:, lo:lo + A_INNER], preferred_element_type=F32)
            for j in range(slabs):
                yj = y[:, j * LANES:(j + 1) * LANES]
                if kind < 2:
                    yj = _rope128(yj, c, sa, sb)
                if kind == 0:
                    yj = yj * Q_SCALE
                cols = slice(kind * A_INNER + j * LANES, kind * A_INNER + (j + 1) * LANES)
                if dil == 1:
                    o_ref[0, :, cols] = yj.astype(BF16)
                    continue
                stage_ref[stage] = yj
                for r in range(dil):
                    rows = stage_ref[stage, pl.ds(r, tm // dil, stride=dil), :]
                    o_ref[r, :, cols] = rows.astype(BF16)
                stage += 1


def _qkv0(x, gain, scale, shift, w, tables, tm=1024):
    bsz, seq, d = x.shape
    n = w.shape[1]
    row = pl.BlockSpec((None, 1, d), lambda b, i: (b, 0, 0))
    tab = pl.BlockSpec((None, tm, LANES), lambda b, i: (b, i, 0))
    width = 3 * A_INNER
    dils = [dil for _, dil in A_GROUPS]
    n_stage = sum(3 * (A_INNER // LANES) for dil in dils if dil > 1)
    return pl.pallas_call(
        functools.partial(_qkv0_kernel, tm=tm),
        grid=(bsz, seq // tm),
        in_specs=[pl.BlockSpec((None, tm, d), lambda b, i: (b, i, 0)),
                  _resident((1, d)), row, row, _resident((d, n)), tab, tab, tab],
        out_specs=[pl.BlockSpec((None, dil, tm // dil, width), lambda b, i: (b, 0, i, 0)) for dil in dils],
        out_shape=[jax.ShapeDtypeStruct((bsz, dil, seq // dil, width), BF16) for dil in dils],
        scratch_shapes=[pltpu.VMEM((n_stage, tm, LANES), F32)],
        compiler_params=_params(2),
        name="qkv0",
    )(x, gain.reshape(1, d), scale, shift, w, *tables)


def _dil_kernel(q_ref, kp_ref, k_ref, vp_ref, v_ref, o_ref, lse_ref, *, tq, dil, unroll):
    first_tile = pl.program_id(1) == 0
    row = lax.broadcasted_iota(jnp.int32, (2 * WIN, 2 * WIN), 0) & (WIN - 1)
    col = lax.broadcasted_iota(jnp.int32, (2 * WIN, 2 * WIN), 1)
    band = (col >= row) & (col <= row + WIN)
    band_first = band & ((col >= WIN) | jnp.logical_not(first_tile))
    lane = lax.broadcasted_iota(jnp.int32, (WIN, LANES), 1)
    low_q = lane < HEAD_DIM
    low_kv = lax.broadcasted_iota(jnp.int32, (2 * WIN, LANES), 1) < HEAD_DIM
    first_head = lax.broadcasted_iota(jnp.int32, (4 * WIN, LANES), 0) < 2 * WIN
    first_lanes = lax.broadcasted_iota(jnp.int32, (4 * WIN, LANES), 1) < HEAD_DIM
    ones2 = jnp.where(first_head == first_lanes, 1.0, 0.0).astype(BF16)

    def residue(r):
        for j in range(tq // WIN):
            rows = slice(j * WIN, (j + 1) * WIN)
            if dil == 1:
                out_rows = rows
            else:
                out_rows = pl.ds(j * WIN * dil + r, WIN, stride=dil)
            for hp in range(A_HEADS // 2):
                cols = slice(hp * LANES, (hp + 1) * LANES)
                q = q_ref[r, rows, cols]
                if j == 0:
                    kk = jnp.concatenate([kp_ref[r, :, cols], k_ref[r, 0:WIN, cols]], axis=0)
                    vv = jnp.concatenate([vp_ref[r, :, cols], v_ref[r, 0:WIN, cols]], axis=0)
                else:
                    kk = k_ref[r, (j - 1) * WIN:(j + 1) * WIN, cols]
                    vv = v_ref[r, (j - 1) * WIN:(j + 1) * WIN, cols]
                zq, zv = jnp.zeros_like(q), jnp.zeros_like(vv)
                q2 = jnp.concatenate([jnp.where(low_q, q, zq), jnp.where(low_q, zq, q)], axis=0)
                s = lax.dot_general(q2, kk, (((1,), (1,)), ((), ())), preferred_element_type=F32)
                s = jnp.where(band_first if j == 0 else band, s, NEG_INF)
                mx = jnp.max(s, axis=-1, keepdims=True)
                pb = jnp.exp2(s - mx).astype(BF16)
                p2 = jnp.concatenate([pb[0:WIN], pb[WIN:2 * WIN]], axis=1)
                v2 = jnp.concatenate([jnp.where(low_kv, vv, zv), jnp.where(low_kv, zv, vv)], axis=0)
                od = jnp.dot(p2, jnp.concatenate([v2, ones2], axis=1), preferred_element_type=F32)
                den = od[:, LANES:]
                o_ref[hp, out_rows, :] = od[:, :LANES] * (1.0 / den)
                mx_lanes = jnp.where(low_q, mx[0:WIN], mx[WIN:2 * WIN])
                lse_ref[hp, out_rows, :] = mx_lanes * LN2 + jnp.log(den)

    if dil == 1:
        residue(0)
    else:
        def body(t, carry):
            for u in range(unroll):
                residue(t * unroll + u)
            return carry
        lax.fori_loop(0, dil // unroll, body, 0)


def _dilated_attention(qkv, dilation, tq, unroll=1):
    bsz, _, length, width = qkv.shape
    seq = length * dilation
    sub = tq // WIN
    span = tq * dilation

    def cur(kind):
        return pl.BlockSpec((None, dilation, tq, A_INNER), lambda b, i: (b, 0, i, kind))

    def prev(kind):
        return pl.BlockSpec((None, dilation, WIN, A_INNER),
                            lambda b, i: (b, 0, jnp.maximum(i * sub - 1, 0), kind))

    return pl.pallas_call(
        functools.partial(_dil_kernel, tq=tq, dil=dilation, unroll=unroll),
        grid=(bsz, length // tq),
        in_specs=[cur(0), prev(1), cur(1), prev(2), cur(2)],
        out_specs=[pl.BlockSpec((None, A_INNER // LANES, span, LANES), lambda b, i: (b, 0, i, 0))] * 2,
        out_shape=[jax.ShapeDtypeStruct((bsz, A_INNER // LANES, seq, LANES), F32)] * 2,
        compiler_params=_params(2),
        name=f"dilated_attn_d{dilation}",
    )(qkv, qkv, qkv, qkv, qkv)


def _mixture(o_refs, l_refs):
    slabs = []
    for hp in range(A_HEADS // 2):
        lses = [l_ref[hp] for l_ref in l_refs]
        m = functools.reduce(jnp.maximum, lses)
        es = [jnp.exp(l - m) for l in lses]
        inv = 1.0 / functools.reduce(lambda a, b: a + b, es)
        mixed = functools.reduce(lambda a, b: a + b, [e * o_ref[hp] for e, o_ref in zip(es, o_refs)])
        slabs.append((mixed * inv).astype(BF16))
    return jnp.concatenate(slabs, axis=1)


def _tail_kernel(*refs, tm, fc, n_groups):
    n_attn = 2 * n_groups if n_groups else 1
    attn_refs = refs[:n_attn]
    (x_ref, wo_ref, gmix_ref, gatemix_ref, gpre_ref, sc_ref, sh_ref, gate_ref, wg_ref, wu_ref,
     cw_ref, cb_ref, wd_ref, gpost_ref, out_ref, a_buf, act_buf) = refs[n_attn:]
    n_chunks = a_buf.shape[0]

    @pl.when(pl.program_id(1) == 0)
    def _():
        for ch in range(n_chunks):
            a_buf[ch, 0:SUBLANES, :] = jnp.zeros((SUBLANES, fc), F32)

    if n_groups:
        attn = _mixture(attn_refs[:n_groups], attn_refs[n_groups:])
    else:
        attn = attn_refs[0][...]
    y_attn = jnp.dot(attn, wo_ref[...], preferred_element_type=F32)
    x = x_ref[...] + gatemix_ref[...] * _rms(y_attn, gmix_ref[...], NORM_EPS)
    h = _rms(x, gpre_ref[...], NORM_EPS) * (1.0 + sc_ref[...]) + sh_ref[...]
    hb = h.astype(BF16)

    def gate_up(ch):
        cols = slice(ch * fc, (ch + 1) * fc)
        a_buf[ch, SUBLANES:SUBLANES + tm, :] = jnp.dot(hb, wg_ref[:, cols], preferred_element_type=F32)
        return jnp.dot(hb, wu_ref[:, cols], preferred_element_type=F32)

    u_next = gate_up(0)
    for ch in range(n_chunks):
        cols = slice(ch * fc, (ch + 1) * fc)
        u = u_next
        if ch + 1 < n_chunks:
            u_next = gate_up(ch + 1)
        conv = cb_ref[:, cols]
        for t in range(CONV_WIDTH):
            start = SUBLANES - (CONV_WIDTH - 1) + t
            conv = conv + cw_ref[t:t + 1, cols] * a_buf[ch, start:start + tm, :]
        inner = conv * (GELU_C1 + GELU_C3 * (conv * conv))
        act = 0.5 * conv * (1.0 + jnp.tanh(inner)) * u
        act_buf[:, cols] = act.astype(BF16)
        a_buf[ch, 0:SUBLANES, :] = a_buf[ch, tm:tm + SUBLANES, :]
    y = jnp.dot(act_buf[...], wd_ref[...], preferred_element_type=F32)
    out_ref[...] = x + gate_ref[...] * _rms(y, gpost_ref[...], NORM_EPS)


def _layer_tail(attn, x, wo, gmix, gate_mix, gpre, scale, shift, gate, wg, wu, cw, cb, wd, gpost,
                tm=512, fc=FFN_CHUNK):
    bsz, seq, d = x.shape
    d_ff = wg.shape[1]
    n_chunks = d_ff // fc
    x_spec = pl.BlockSpec((None, tm, d), lambda b, i: (b, i, 0))
    row = pl.BlockSpec((None, 1, d), lambda b, i: (b, 0, 0))
    if isinstance(attn, tuple):
        outs, lses = attn
        n_groups = len(outs)
        attn_args = [*outs, *lses]
        attn_specs = [pl.BlockSpec((None, A_INNER // LANES, tm, LANES),
                                   lambda b, i: (b, 0, i, 0))] * (2 * n_groups)
    else:
        n_groups = 0
        attn_args = [attn]
        attn_specs = [x_spec]
    return pl.pallas_call(
        functools.partial(_tail_kernel, tm=tm, fc=fc, n_groups=n_groups),
        grid=(bsz, seq // tm),
        in_specs=attn_specs + [
            x_spec, _resident(wo.shape), _resident((1, d)), row,
            _resident((1, d)), row, row, row,
            _resident((d, d_ff)), _resident((d, d_ff)), _resident((CONV_WIDTH, d_ff)),
            _resident((1, d_ff)), _resident((d_ff, d)), _resident((1, d))],
        out_specs=x_spec,
        out_shape=jax.ShapeDtypeStruct((bsz, seq, d), F32),
        scratch_shapes=[pltpu.VMEM((n_chunks, tm + SUBLANES, fc), F32), pltpu.VMEM((tm, d_ff), BF16)],
        compiler_params=_params(2, sequential=True),
        name="layer_tail",
    )(*attn_args, x, wo, gmix.reshape(1, d), gate_mix, gpre.reshape(1, d), scale, shift, gate,
      wg, wu, cw, cb.reshape(1, d_ff), wd, gpost.reshape(1, d))


def _qkv1_kernel(x_ref, gq_ref, scq_ref, shq_ref, gkv_ref, sckv_ref, shkv_ref, wq_ref, wk_ref, wvt_ref,
                 c_ref, sa_ref, sb_ref, q_ref, k_ref, vt_ref):
    x = x_ref[...]
    rstd = lax.rsqrt(jnp.mean(x * x, axis=-1, keepdims=True) + NORM_EPS)
    xn = x * rstd
    hq = ((xn * gq_ref[...]) * (1.0 + scq_ref[...]) + shq_ref[...]).astype(BF16)
    hkv = ((xn * gkv_ref[...]) * (1.0 + sckv_ref[...]) + shkv_ref[...]).astype(BF16)
    c, sa, sb = c_ref[...], sa_ref[...], sb_ref[...]
    d = x.shape[1]
    for lo in range(0, d, PROJ_CHUNK):
        yq = jnp.dot(hq, wq_ref[:, lo:lo + PROJ_CHUNK], preferred_element_type=F32)
        yk = jnp.dot(hkv, wk_ref[:, lo:lo + PROJ_CHUNK], preferred_element_type=F32)
        for j in range(PROJ_CHUNK // LANES):
            cols = slice(lo + j * LANES, lo + (j + 1) * LANES)
            sub = slice(j * LANES, (j + 1) * LANES)
            q_ref[:, cols] = (_rope128(yq[:, sub], c, sa, sb) * Q_SCALE).astype(BF16)
            k_ref[:, cols] = _rope128(yk[:, sub], c, sa, sb).astype(BF16)
    for lo in range(0, d, VT_CHUNK):
        vt = lax.dot_general(wvt_ref[lo:lo + VT_CHUNK, :], hkv, (((1,), (1,)), ((), ())),
                             preferred_element_type=F32)
        vt_ref[lo:lo + VT_CHUNK, :] = vt.astype(BF16)


def _qkv1(x, gq, scq, shq, gkv, sckv, shkv, wq, wk, wvt, tables, tm):
    bsz, seq, d = x.shape
    x_spec = pl.BlockSpec((None, tm, d), lambda b, i: (b, i, 0))
    row = pl.BlockSpec((None, 1, d), lambda b, i: (b, 0, 0))
    tab = pl.BlockSpec((None, tm, LANES), lambda b, i: (b, i, 0))
    shape = jax.ShapeDtypeStruct((bsz, seq, d), BF16)
    return pl.pallas_call(
        _qkv1_kernel,
        grid=(bsz, seq // tm),
        in_specs=[x_spec, _resident((1, d)), row, row, _resident((1, d)), row, row,
                  _resident((d, d)), _resident((d, d)), _resident((d, d)), tab, tab, tab],
        out_specs=[x_spec, x_spec, pl.BlockSpec((None, None, d, tm), lambda b, i: (b, i, 0, 0))],
        out_shape=[shape, shape, jax.ShapeDtypeStruct((bsz, seq // tm, d, tm), BF16)],
        compiler_params=_params(2),
        name="qkv1",
    )(x, gq.reshape(1, d), scq, shq, gkv.reshape(1, d), sckv, shkv, wq, wk, wvt, *tables)


def _diff_kernel(lq1_ref, lk1_ref, lq2_ref, lk2_ref, g_ref, q_ref, k_ref, vt_ref, o_ref,
                 *scratch, tq, lam_init):
    lam = (jnp.exp(jnp.sum(lq1_ref[...] * lk1_ref[...], axis=-1, keepdims=True))
           - jnp.exp(jnp.sum(lq2_ref[...] * lk2_ref[...], axis=-1, keepdims=True)) + lam_init)
    scale = g_ref[...] * (1.0 - lam_init)

    def tile(i, carry):
        _diff_tile(i, lam, scale, q_ref, k_ref, vt_ref, o_ref, *scratch, tq=tq)
        return carry

    lax.fori_loop(0, q_ref.shape[0] // tq, tile, 0)


def _diff_tile(i, lam, scale, q_ref, k_ref, vt_ref, o_ref, acc_ref, m_ref, s_ref, bmax_ref, *, tq):
    rows = pl.ds(pl.multiple_of(i * tq, tq), tq)
    q = q_ref[rows, :]
    lane = lax.broadcasted_iota(jnp.int32, q.shape, 1)
    zero = jnp.zeros_like(q)
    q_maps = (jnp.where(lane < HEAD_DIM, q, zero), jnp.where(lane >= HEAD_DIM, q, zero))
    acc_ref[...] = jnp.zeros(acc_ref.shape, F32)
    m_ref[...] = jnp.full(m_ref.shape, NEG_INF, F32)

    half = tq // 2
    full = (0, tq, 0, tq)
    diag_a = (0, half, 0, tq)
    diag_b = (half, half, half, half)

    def scores(jb, slot, part, masked):
        k0, nk, q0, nq = part
        start = jb * tq + k0
        if not isinstance(jb, int):
            start = pl.multiple_of(start, half)
        k = k_ref[pl.ds(start, nk), :]
        for c in range(2):
            s = lax.dot_general(k, q_maps[c][q0:q0 + nq, :], (((1,), (1,)), ((), ())),
                                preferred_element_type=F32)
            if masked:
                key = lax.broadcasted_iota(jnp.int32, (nk, nq), 0) + k0
                qry = lax.broadcasted_iota(jnp.int32, (nk, nq), 1) + q0
                s = jnp.where(key <= qry, s, NEG_INF)
            s_ref[slot, c, 0:nk, q0:q0 + nq] = s
            bmax_ref[slot, c, :, q0:q0 + nq] = jnp.max(s, axis=0, keepdims=True)

    def update(jb, slot, part):
        k0, nk, q0, nq = part
        lanes = slice(q0, q0 + nq)
        vt = jnp.concatenate([vt_ref[jb, :, k0:k0 + nk], jnp.ones((ONES_ROWS, nk), BF16)], axis=0)
        for c in range(2):
            s = s_ref[slot, c, 0:nk, lanes]
            m_old = m_ref[c, :, lanes]
            m_new = jnp.maximum(m_old, bmax_ref[slot, c, :, lanes])
            alpha = jnp.exp2(m_old - m_new)
            p = jnp.exp2(s - m_new)
            acc_ref[c, :, lanes] = (alpha * acc_ref[c, :, lanes]
                                    + jnp.dot(vt, p.astype(BF16), preferred_element_type=F32))
            m_ref[c, :, lanes] = m_new

    def diagonal(prev_slot):
        a_slot = 0 if prev_slot is None else 1 - prev_slot
        scores(i, a_slot, diag_a, True)
        if prev_slot is not None:
            update(i - 1, prev_slot, full)
        scores(i, 1 - a_slot, diag_b, True)
        update(i, a_slot, diag_a)
        update(i, 1 - a_slot, diag_b)

    odd = (i & 1) == 1

    @pl.when(i == 0)
    def _():
        diagonal(None)

    @pl.when(i > 0)
    def _():
        scores(0, 0, full, False)

    def body(t, carry):
        for slot in range(2):
            scores(2 * t + slot + 1, 1 - slot, full, False)
            update(2 * t + slot, slot, full)
        return carry

    lax.fori_loop(0, lax.shift_right_arithmetic(i - 1, 1), body, 0)

    @pl.when(odd)
    def _():
        diagonal(0)

    @pl.when(jnp.logical_and(i > 0, jnp.logical_not(odd)))
    def _():
        scores(i - 1, 1, full, False)
        update(i - 2, 0, full)
        diagonal(1)

    width = 2 * HEAD_DIM
    outs = [acc_ref[c, 0:width, :] * (1.0 / acc_ref[c, width:width + 1, :]) for c in range(2)]
    o_t = outs[0] - lam * outs[1]
    ms = jnp.mean(o_t * o_t, axis=0, keepdims=True)
    o_t = o_t * lax.rsqrt(ms + SUBLN_EPS) * scale
    o_ref[rows, :] = o_t.T.astype(BF16)


def _diff_attention(q, k, vt, lq1, lk1, lq2, lk2, subln, lam_init, tq):
    bsz, seq, d = q.shape
    width = 2 * HEAD_DIM
    vec = _resident((1, HEAD_DIM))
    head_spec = pl.BlockSpec((None, seq, width), lambda b, h: (b, 0, h))
    vt_spec = pl.BlockSpec((None, seq // tq, width, tq), lambda b, h: (b, 0, h, 0))
    return pl.pallas_call(
        functools.partial(_diff_kernel, tq=tq, lam_init=lam_init),
        grid=(bsz, B_HEADS),
        in_specs=[vec, vec, vec, vec, _resident((width, 1)), head_spec, head_spec, vt_spec],
        out_specs=head_spec,
        out_shape=jax.ShapeDtypeStruct((bsz, seq, d), BF16),
        scratch_shapes=[pltpu.VMEM((2, width + ONES_ROWS, tq), F32), pltpu.VMEM((2, 1, tq), F32),
                        pltpu.VMEM((2, 2, tq, tq), F32),
                        pltpu.VMEM((2, 2, 1, tq), F32)],
        compiler_params=_params(2),
        name="diff_attn",
    )(lq1.reshape(1, HEAD_DIM), lk1.reshape(1, HEAD_DIM), lq2.reshape(1, HEAD_DIM),
      lk2.reshape(1, HEAD_DIM), subln.reshape(width, 1), q, k, vt)


def _split3(m):
    return jnp.split(m[:, None, :], 3, axis=-1)


def kernel(x, c, positions, mod_mix_w, mod_mix_b, mod_ffn_w, mod_ffn_b, norm_pre_mix, norm_post_mix, norm_pre_ffn, norm_post_ffn, ffn_w_gate, ffn_w_up, ffn_conv_w, ffn_conv_b, ffn_w_down, a_w_qkv, a_w_o, kv_norm, kv_mod_w, kv_mod_b, b_w_k, b_w_v, b_w_q, b_lambda_q1, b_lambda_k1, b_lambda_q2, b_lambda_k2, b_subln, b_w_o):
    mix_mod = _ada_mod(c, mod_mix_w, mod_mix_b)
    ffn_mod = _ada_mod(c, mod_ffn_w, mod_ffn_b)
    kv_mod = _ada_mod(c, kv_mod_w[None], kv_mod_b[None])[0]
    tables = _rope_tables(positions)

    def tail(attn, x, w_o, gate_mix, l):
        shift, scale, gate = _split3(ffn_mod[l])
        return _layer_tail(attn, x, w_o.astype(BF16), norm_post_mix[l], gate_mix,
                           norm_pre_ffn[l], scale, shift, gate, ffn_w_gate[l].astype(BF16),
                           ffn_w_up[l].astype(BF16), ffn_conv_w[l], ffn_conv_b[l],
                           ffn_w_down[l].astype(BF16), norm_post_ffn[l])

    shift, scale, gate = _split3(mix_mod[0])
    qkv_groups = _qkv0(x, norm_pre_mix[0], scale, shift, a_w_qkv[0].astype(BF16), tables)
    outs, lses = [], []
    for qkv, (_, dilation), (tq, unroll) in zip(qkv_groups, A_GROUPS, DIL_TILES):
        o, lse = _dilated_attention(qkv, dilation, tq, unroll)
        outs.append(o)
        lses.append(lse)
    x = tail((outs, lses), x, a_w_o[0], gate, 0)

    shift, scale, gate = _split3(mix_mod[1])
    kv_shift, kv_scale = jnp.split(kv_mod[:, None, :], 2, axis=-1)
    q, k, vt = _qkv1(x, norm_pre_mix[1], scale, shift, kv_norm, kv_scale, kv_shift,
                     b_w_q[0].astype(BF16), b_w_k.astype(BF16), b_w_v.T.astype(BF16), tables,
                     tm=DIFF_TILE)
    lam_init = 0.8 - 0.6 * math.exp(-0.3 * 1)
    o = _diff_attention(q, k, vt, b_lambda_q1[0], b_lambda_k1[0], b_lambda_q2[0], b_lambda_k2[0],
                        b_subln[0], lam_init, tq=DIFF_TILE)
    x = tail(o, x, b_w_o[0], gate, 1)
    return x
```

```python
import functools
import math

import jax
import jax.numpy as jnp
from jax import lax
from jax.experimental import pallas as pl
from jax.experimental.pallas import tpu as pltpu

D_MODEL = 1024
HEAD_DIM = 64
ROT_DIM = HEAD_DIM // 4
ROT_HALF = ROT_DIM // 2
ROPE_THETA = 500000.0
A_GROUPS = ((128, 1), (512, 4), (2048, 16))
A_HEADS = 8
A_INNER = A_HEADS * HEAD_DIM
B_HEADS = D_MODEL // (2 * HEAD_DIM)
CONV_WIDTH = 3
NORM_EPS = 1e-6
SUBLN_EPS = 1e-5
NEG_INF = -1e30
LOG2E = math.log2(math.e)
LN2 = math.log(2.0)
Q_SCALE = HEAD_DIM ** -0.5 * LOG2E
GELU_C1 = math.sqrt(2.0 / math.pi)
GELU_C3 = GELU_C1 * 0.044715

LANES = 128
SUBLANES = 8
WIN = 128
PROJ_CHUNK = 512
VT_CHUNK = 256
FFN_CHUNK = 256
DIFF_TILE = 512
ONES_ROWS = 16
DIL_TILES = ((512, 1), (256, 2), (128, 4))
VMEM_LIMIT = 56 * 1024 * 1024

F32 = jnp.float32
BF16 = jnp.bfloat16


def _resident(shape):
    zeros = (0,) * len(shape)
    return pl.BlockSpec(shape, lambda *_: zeros, pipeline_mode=pl.Buffered(1))


def _params(n_axes, sequential=False):
    sem = ("arbitrary",) * n_axes if sequential else ("parallel",) * n_axes
    return pltpu.CompilerParams(dimension_semantics=sem, vmem_limit_bytes=VMEM_LIMIT)


def _normed(x, eps):
    return x * lax.rsqrt(jnp.mean(x * x, axis=-1, keepdims=True) + eps)


def _rope128(y, c, sa, sb):
    return y * c + pltpu.roll(y, LANES - ROT_HALF, 1) * sa + pltpu.roll(y, ROT_HALF, 1) * sb


def _mod_kernel(c_ref, w_ref, b_ref, o_ref):
    c = c_ref[...]
    c_act = c / (1.0 + jnp.exp(-c))
    o_ref[...] = jnp.dot(c_act, w_ref[...], preferred_element_type=F32) + b_ref[...]


def _ada_mod(c, w, b, tn=1024):
    n_layers, d, n = w.shape
    bsz = c.shape[0]
    return pl.pallas_call(
        _mod_kernel,
        grid=(n_layers, n // tn),
        in_specs=[
            pl.BlockSpec((bsz, d), lambda l, j: (0, 0)),
            pl.BlockSpec((None, d, tn), lambda l, j: (l, 0, j)),
            pl.BlockSpec((None, 1, tn), lambda l, j: (l, 0, j)),
        ],
        out_specs=pl.BlockSpec((None, bsz, tn), lambda l, j: (l, 0, j)),
        out_shape=jax.ShapeDtypeStruct((n_layers, bsz, n), F32),
        compiler_params=_params(2),
        name="ada_mod",
    )(c, w, b.reshape(n_layers, 1, n))


def _rope_kernel(pos_ref, inv_ref, c_ref, sa_ref, sb_ref):
    ang = pos_ref[...].astype(F32) * inv_ref[...]
    cos = jnp.cos(ang)
    sin = jnp.sin(ang)
    lane = lax.broadcasted_iota(jnp.int32, ang.shape, 1) & (HEAD_DIM - 1)
    c_ref[...] = cos
    sa_ref[...] = jnp.where(lane < ROT_HALF, -sin, 0.0)
    sb_ref[...] = jnp.where((lane >= ROT_HALF) & (lane < ROT_DIM), sin, 0.0)


def _rope_tables(positions, ts=1024):
    bsz, seq = positions.shape
    inv = jnp.power(ROPE_THETA, -jnp.arange(0, ROT_DIM, 2, dtype=F32) / ROT_DIM)
    head = jnp.concatenate([inv, inv, jnp.zeros((HEAD_DIM - ROT_DIM,), F32)])
    inv_lane = jnp.tile(head, LANES // HEAD_DIM).reshape(1, LANES)
    spec = pl.BlockSpec((None, ts, LANES), lambda b, i: (b, i, 0))
    shape = jax.ShapeDtypeStruct((bsz, seq, LANES), F32)
    return pl.pallas_call(
        _rope_kernel,
        grid=(bsz, seq // ts),
        in_specs=[pl.BlockSpec((None, ts, 1), lambda b, i: (b, i, 0)),
                  pl.BlockSpec((1, LANES), lambda b, i: (0, 0))],
        out_specs=[spec, spec, spec],
        out_shape=[shape, shape, shape],
        compiler_params=_params(2),
        name="rope_tables",
    )(positions.reshape(bsz, seq, 1), inv_lane)


def _qkv0_kernel(x_ref, g_ref, sc_ref, sh_ref, w_ref, c_ref, sa_ref, sb_ref, *rest, tm):
    out_refs, stage_ref = rest[:-1], rest[-1]
    h = _normed(x_ref[...], NORM_EPS) * (g_ref[...] * (1.0 + sc_ref[...])) + sh_ref[...]
    hb = h.astype(BF16)
    c, sa, sb = c_ref[...], sa_ref[...], sb_ref[...]
    slabs = A_INNER // LANES
    stage = 0
    for grp, (_, dil) in reversed(list(enumerate(A_GROUPS))):
        o_ref = out_refs[grp]
        for kind in range(3):
            lo = (grp * 3 + kind) * A_INNER
            y = jnp.dot(hb, w_ref[:, lo:lo + A_INNER], preferred_element_type=F32)
            for j in range(slabs):
                yj = y[:, j * LANES:(j + 1) * LANES]
                if kind < 2:
                    yj = _rope128(yj, c, sa, sb)
                if kind == 0:
                    yj = yj * Q_SCALE
                cols = slice(kind * A_INNER + j * LANES, kind * A_INNER + (j + 1) * LANES)
                if dil == 1:
                    o_ref[0, :, cols] = yj.astype(BF16)
                    continue
                stage_ref[stage] = yj
                for r in range(dil):
                    rows = stage_ref[stage, pl.ds(r, tm // dil, stride=dil), :]
                    o_ref[r, :, cols] = rows.astype(BF16)
                stage += 1


def _qkv0(x, gain, scale, shift, w, tables, tm=1024):
    bsz, seq, d = x.shape
    n = w.shape[1]
    row = pl.BlockSpec((None, 1, d), lambda b, i: (b, 0, 0))
    tab = pl.BlockSpec((None, tm, LANES), lambda b, i: (b, i, 0))
    width = 3 * A_INNER
    dils = [dil for _, dil in A_GROUPS]
    n_stage = sum(3 * (A_INNER // LANES) for dil in dils if dil > 1)
    return pl.pallas_call(
        functools.partial(_qkv0_kernel, tm=tm),
        grid=(bsz, seq // tm),
        in_specs=[pl.BlockSpec((None, tm, d), lambda b, i: (b, i, 0)),
                  _resident((1, d)), row, row, _resident((d, n)), tab, tab, tab],
        out_specs=[pl.BlockSpec((None, dil, tm // dil, width), lambda b, i: (b, 0, i, 0)) for dil in dils],
        out_shape=[jax.ShapeDtypeStruct((bsz, dil, seq // dil, width), BF16) for dil in dils],
        scratch_shapes=[pltpu.VMEM((n_stage, tm, LANES), F32)],
        compiler_params=_params(2),
        name="qkv0",
    )(x, gain.reshape(1, d), scale, shift, w, *tables)


def _dil_kernel(q_ref, kp_ref, k_ref, vp_ref, v_ref, o_ref, lse_ref, *, tq, dil, unroll):
    first_tile = pl.program_id(1) == 0
    row = lax.broadcasted_iota(jnp.int32, (2 * WIN, 2 * WIN), 0) & (WIN - 1)
    col = lax.broadcasted_iota(jnp.int32, (2 * WIN, 2 * WIN), 1)
    band = (col >= row) & (col <= row + WIN)
    band_first = band & ((col >= WIN) | jnp.logical_not(first_tile))
    lane = lax.broadcasted_iota(jnp.int32, (WIN, LANES), 1)
    low_q = lane < HEAD_DIM
    low_kv = lax.broadcasted_iota(jnp.int32, (2 * WIN, LANES), 1) < HEAD_DIM
    first_head = lax.broadcasted_iota(jnp.int32, (4 * WIN, LANES), 0) < 2 * WIN
    first_lanes = lax.broadcasted_iota(jnp.int32, (4 * WIN, LANES), 1) < HEAD_DIM
    ones2 = jnp.where(first_head == first_lanes, 1.0, 0.0).astype(BF16)

    def residue(r):
        for j in range(tq // WIN):
            rows = slice(j * WIN, (j + 1) * WIN)
            if dil == 1:
                out_rows = rows
            else:
                out_rows = pl.ds(j * WIN * dil + r, WIN, stride=dil)
            for hp in range(A_HEADS // 2):
                cols = slice(hp * LANES, (hp + 1) * LANES)
                q = q_ref[r, rows, cols]
                if j == 0:
                    kk = jnp.concatenate([kp_ref[r, :, cols], k_ref[r, 0:WIN, cols]], axis=0)
                    vv = jnp.concatenate([vp_ref[r, :, cols], v_ref[r, 0:WIN, cols]], axis=0)
                else:
                    kk = k_ref[r, (j - 1) * WIN:(j + 1) * WIN, cols]
                    vv = v_ref[r, (j - 1) * WIN:(j + 1) * WIN, cols]
                zq, zv = jnp.zeros_like(q), jnp.zeros_like(vv)
                q2 = jnp.concatenate([jnp.where(low_q, q, zq), jnp.where(low_q, zq, q)], axis=0)
                s = lax.dot_general(q2, kk, (((1,), (1,)), ((), ())), preferred_element_type=F32)
                s = jnp.where(band_first if j == 0 else band, s, NEG_INF)
                mx = jnp.max(s, axis=-1, keepdims=True)
                pb = jnp.exp2(s - mx).astype(BF16)
                p2 = jnp.concatenate([pb[0:WIN], pb[WIN:2 * WIN]], axis=1)
                v2 = jnp.concatenate([jnp.where(low_kv, vv, zv), jnp.where(low_kv, zv, vv)], axis=0)
                od = jnp.dot(p2, jnp.concatenate([v2, ones2], axis=1), preferred_element_type=F32)
                den = od[:, LANES:]
                o_ref[hp, out_rows, :] = od[:, :LANES] * (1.0 / den)
                mx_lanes = jnp.where(low_q, mx[0:WIN], mx[WIN:2 * WIN])
                lse_ref[hp, out_rows, :] = mx_lanes * LN2 + jnp.log(den)

    if dil == 1:
        residue(0)
    else:
        def body(t, carry):
            for u in range(unroll):
                residue(t * unroll + u)
            return carry
        lax.fori_loop(0, dil // unroll, body, 0)


def _dilated_attention(qkv, dilation, tq, unroll=1):
    bsz, _, length, width = qkv.shape
    seq = length * dilation
    sub = tq // WIN
    span = tq * dilation

    def cur(kind):
        return pl.BlockSpec((None, dilation, tq, A_INNER), lambda b, i: (b, 0, i, kind))

    def prev(kind):
        return pl.BlockSpec((None, dilation, WIN, A_INNER),
                            lambda b, i: (b, 0, jnp.maximum(i * sub - 1, 0), kind))

    return pl.pallas_call(
        functools.partial(_dil_kernel, tq=tq, dil=dilation, unroll=unroll),
        grid=(bsz, length // tq),
        in_specs=[cur(0), prev(1), cur(1), prev(2), cur(2)],
        out_specs=[pl.BlockSpec((None, A_INNER // LANES, span, LANES), lambda b, i: (b, 0, i, 0))] * 2,
        out_shape=[jax.ShapeDtypeStruct((bsz, A_INNER // LANES, seq, LANES), F32)] * 2,
        compiler_params=_params(2),
        name=f"dilated_attn_d{dilation}",
    )(qkv, qkv, qkv, qkv, qkv)


def _mixture(o_refs, l_refs):
    slabs = []
    for hp in range(A_HEADS // 2):
        lses = [l_ref[hp] for l_ref in l_refs]
        m = functools.reduce(jnp.maximum, lses)
        es = [jnp.exp(l - m) for l in lses]
        inv = 1.0 / functools.reduce(lambda a, b: a + b, es)
        mixed = functools.reduce(lambda a, b: a + b, [e * o_ref[hp] for e, o_ref in zip(es, o_refs)])
        slabs.append((mixed * inv).astype(BF16))
    return jnp.concatenate(slabs, axis=1)


def _tail_kernel(*refs, tm, fc, n_groups):
    n_attn = 2 * n_groups if n_groups else 1
    attn_refs = refs[:n_attn]
    (x_ref, wo_ref, gmix_ref, gatemix_ref, gpre_ref, sc_ref, sh_ref, gate_ref, wg_ref, wu_ref,
     cw_ref, cb_ref, wd_ref, gpost_ref, out_ref, a_buf, act_buf) = refs[n_attn:]
    n_chunks = a_buf.shape[0]

    @pl.when(pl.program_id(1) == 0)
    def _():
        for ch in range(n_chunks):
            a_buf[ch, 0:SUBLANES, :] = jnp.zeros((SUBLANES, fc), F32)

    if n_groups:
        attn = _mixture(attn_refs[:n_groups], attn_refs[n_groups:])
    else:
        attn = attn_refs[0][...]
    y_attn = jnp.dot(attn, wo_ref[...], preferred_element_type=F32)
    x = x_ref[...] + _normed(y_attn, NORM_EPS) * (gatemix_ref[...] * gmix_ref[...])
    h = _normed(x, NORM_EPS) * (gpre_ref[...] * (1.0 + sc_ref[...])) + sh_ref[...]
    hb = h.astype(BF16)

    def gate_up(ch):
        cols = slice(ch * fc, (ch + 1) * fc)
        a_buf[ch, SUBLANES:SUBLANES + tm, :] = jnp.dot(hb, wg_ref[:, cols], preferred_element_type=F32)
        return jnp.dot(hb, wu_ref[:, cols], preferred_element_type=F32)

    u_next = gate_up(0)
    for ch in range(n_chunks):
        cols = slice(ch * fc, (ch + 1) * fc)
        u = u_next
        if ch + 1 < n_chunks:
            u_next = gate_up(ch + 1)
        conv = cb_ref[:, cols]
        for t in range(CONV_WIDTH):
            start = SUBLANES - (CONV_WIDTH - 1) + t
            conv = conv + cw_ref[t:t + 1, cols] * a_buf[ch, start:start + tm, :]
        inner = conv * (GELU_C1 + GELU_C3 * (conv * conv))
        act = conv * (1.0 + jnp.tanh(inner)) * u
        act_buf[:, cols] = act.astype(BF16)
        a_buf[ch, 0:SUBLANES, :] = a_buf[ch, tm:tm + SUBLANES, :]
    y = jnp.dot(act_buf[...], wd_ref[...], preferred_element_type=F32)
    out_ref[...] = x + _normed(y, NORM_EPS) * (gate_ref[...] * gpost_ref[...])


def _layer_tail(attn, x, wo, gmix, gate_mix, gpre, scale, shift, gate, wg, wu, cw, cb, wd, gpost,
                tm=512, fc=FFN_CHUNK):
    bsz, seq, d = x.shape
    d_ff = wg.shape[1]
    n_chunks = d_ff // fc
    x_spec = pl.BlockSpec((None, tm, d), lambda b, i: (b, i, 0))
    row = pl.BlockSpec((None, 1, d), lambda b, i: (b, 0, 0))
    if isinstance(attn, tuple):
        outs, lses = attn
        n_groups = len(outs)
        attn_args = [*outs, *lses]
        attn_specs = [pl.BlockSpec((None, A_INNER // LANES, tm, LANES),
                                   lambda b, i: (b, 0, i, 0))] * (2 * n_groups)
    else:
        n_groups = 0
        attn_args = [attn]
        attn_specs = [x_spec]
    return pl.pallas_call(
        functools.partial(_tail_kernel, tm=tm, fc=fc, n_groups=n_groups),
        grid=(bsz, seq // tm),
        in_specs=attn_specs + [
            x_spec, _resident(wo.shape), _resident((1, d)), row,
            _resident((1, d)), row, row, row,
            _resident((d, d_ff)), _resident((d, d_ff)), _resident((CONV_WIDTH, d_ff)),
            _resident((1, d_ff)), _resident((d_ff, d)), _resident((1, d))],
        out_specs=x_spec,
        out_shape=jax.ShapeDtypeStruct((bsz, seq, d), F32),
        scratch_shapes=[pltpu.VMEM((n_chunks, tm + SUBLANES, fc), F32), pltpu.VMEM((tm, d_ff), BF16)],
        compiler_params=_params(2, sequential=True),
        name="layer_tail",
    )(*attn_args, x, wo, gmix.reshape(1, d), gate_mix, gpre.reshape(1, d), scale, shift, gate,
      wg, wu, cw, cb.reshape(1, d_ff), wd, gpost.reshape(1, d))


def _qkv1_kernel(x_ref, gq_ref, scq_ref, shq_ref, gkv_ref, sckv_ref, shkv_ref, wq_ref, wk_ref, wvt_ref,
                 c_ref, sa_ref, sb_ref, q_ref, k_ref, vt_ref):
    x = x_ref[...]
    xn = _normed(x, NORM_EPS)
    hq = (xn * (gq_ref[...] * (1.0 + scq_ref[...])) + shq_ref[...]).astype(BF16)
    hkv = (xn * (gkv_ref[...] * (1.0 + sckv_ref[...])) + shkv_ref[...]).astype(BF16)
    c, sa, sb = c_ref[...], sa_ref[...], sb_ref[...]
    d = x.shape[1]
    for lo in range(0, d, PROJ_CHUNK):
        yq = jnp.dot(hq, wq_ref[:, lo:lo + PROJ_CHUNK], preferred_element_type=F32)
        yk = jnp.dot(hkv, wk_ref[:, lo:lo + PROJ_CHUNK], preferred_element_type=F32)
        for j in range(PROJ_CHUNK // LANES):
            cols = slice(lo + j * LANES, lo + (j + 1) * LANES)
            sub = slice(j * LANES, (j + 1) * LANES)
            q_ref[:, cols] = (_rope128(yq[:, sub], c, sa, sb) * Q_SCALE).astype(BF16)
            k_ref[:, cols] = _rope128(yk[:, sub], c, sa, sb).astype(BF16)
    for lo in range(0, d, VT_CHUNK):
        vt = lax.dot_general(wvt_ref[lo:lo + VT_CHUNK, :], hkv, (((1,), (1,)), ((), ())),
                             preferred_element_type=F32)
        vt_ref[lo:lo + VT_CHUNK, :] = vt.astype(BF16)


def _qkv1(x, gq, scq, shq, gkv, sckv, shkv, wq, wk, wvt, tables, tm):
    bsz, seq, d = x.shape
    x_spec = pl.BlockSpec((None, tm, d), lambda b, i: (b, i, 0))
    row = pl.BlockSpec((None, 1, d), lambda b, i: (b, 0, 0))
    tab = pl.BlockSpec((None, tm, LANES), lambda b, i: (b, i, 0))
    shape = jax.ShapeDtypeStruct((bsz, seq, d), BF16)
    return pl.pallas_call(
        _qkv1_kernel,
        grid=(bsz, seq // tm),
        in_specs=[x_spec, _resident((1, d)), row, row, _resident((1, d)), row, row,
                  _resident((d, d)), _resident((d, d)), _resident((d, d)), tab, tab, tab],
        out_specs=[x_spec, x_spec, pl.BlockSpec((None, None, d, tm), lambda b, i: (b, i, 0, 0))],
        out_shape=[shape, shape, jax.ShapeDtypeStruct((bsz, seq // tm, d, tm), BF16)],
        compiler_params=_params(2),
        name="qkv1",
    )(x, gq.reshape(1, d), scq, shq, gkv.reshape(1, d), sckv, shkv, wq, wk, wvt, *tables)


def _diff_kernel(lq1_ref, lk1_ref, lq2_ref, lk2_ref, g_ref, q_ref, k_ref, vt_ref, o_ref,
                 *scratch, tq, lam_init):
    lam = (jnp.exp(jnp.sum(lq1_ref[...] * lk1_ref[...], axis=-1, keepdims=True))
           - jnp.exp(jnp.sum(lq2_ref[...] * lk2_ref[...], axis=-1, keepdims=True)) + lam_init)
    scale = g_ref[...] * (1.0 - lam_init)

    def tile(i, carry):
        _diff_tile(i, lam, scale, q_ref, k_ref, vt_ref, o_ref, *scratch, tq=tq)
        return carry

    lax.fori_loop(0, q_ref.shape[0] // tq, tile, 0)


def _diff_tile(i, lam, scale, q_ref, k_ref, vt_ref, o_ref, acc_ref, m_ref, s_ref, bmax_ref, *, tq):
    rows = pl.ds(pl.multiple_of(i * tq, tq), tq)
    q = q_ref[rows, :]
    lane = lax.broadcasted_iota(jnp.int32, q.shape, 1)
    zero = jnp.zeros_like(q)
    q_maps = (jnp.where(lane < HEAD_DIM, q, zero), jnp.where(lane >= HEAD_DIM, q, zero))
    acc_ref[...] = jnp.zeros(acc_ref.shape, F32)
    m_ref[...] = jnp.full(m_ref.shape, NEG_INF, F32)

    half = tq // 2
    full = (0, tq, 0, tq)
    diag_a = (0, half, 0, tq)
    diag_b = (half, half, half, half)

    def scores(jb, slot, part, masked):
        k0, nk, q0, nq = part
        start = jb * tq + k0
        if not isinstance(jb, int):
            start = pl.multiple_of(start, half)
        k = k_ref[pl.ds(start, nk), :]
        for c in range(2):
            s = lax.dot_general(k, q_maps[c][q0:q0 + nq, :], (((1,), (1,)), ((), ())),
                                preferred_element_type=F32)
            if masked:
                key = lax.broadcasted_iota(jnp.int32, (nk, nq), 0) + k0
                qry = lax.broadcasted_iota(jnp.int32, (nk, nq), 1) + q0
                s = jnp.where(key <= qry, s, NEG_INF)
            s_ref[slot, c, 0:nk, q0:q0 + nq] = s
            bmax_ref[slot, c, :, q0:q0 + nq] = jnp.max(s, axis=0, keepdims=True)

    def update(jb, slot, part):
        k0, nk, q0, nq = part
        lanes = slice(q0, q0 + nq)
        vt = jnp.concatenate([vt_ref[jb, :, k0:k0 + nk], jnp.ones((ONES_ROWS, nk), BF16)], axis=0)
        for c in range(2):
            s = s_ref[slot, c, 0:nk, lanes]
            m_old = m_ref[c, :, lanes]
            m_new = jnp.maximum(m_old, bmax_ref[slot, c, :, lanes])
            alpha = jnp.exp2(m_old - m_new)
            p = jnp.exp2(s - m_new)
            acc_ref[c, :, lanes] = (alpha * acc_ref[c, :, lanes]
                                    + jnp.dot(vt, p.astype(BF16), preferred_element_type=F32))
            m_ref[c, :, lanes] = m_new

    def diagonal(prev_slot):
        a_slot = 0 if prev_slot is None else 1 - prev_slot
        scores(i, a_slot, diag_a, True)
        if prev_slot is not None:
            update(i - 1, prev_slot, full)
        scores(i, 1 - a_slot, diag_b, True)
        update(i, a_slot, diag_a)
        update(i, 1 - a_slot, diag_b)

    odd = (i & 1) == 1

    @pl.when(i == 0)
    def _():
        diagonal(None)

    @pl.when(i > 0)
    def _():
        scores(0, 0, full, False)

    def body(t, carry):
        for slot in range(2):
            scores(2 * t + slot + 1, 1 - slot, full, False)
            update(2 * t + slot, slot, full)
        return carry

    lax.fori_loop(0, lax.shift_right_arithmetic(i - 1, 1), body, 0)

    @pl.when(odd)
    def _():
        diagonal(0)

    @pl.when(jnp.logical_and(i > 0, jnp.logical_not(odd)))
    def _():
        scores(i - 1, 1, full, False)
        update(i - 2, 0, full)
        diagonal(1)

    width = 2 * HEAD_DIM
    outs = [acc_ref[c, 0:width, :] * (1.0 / acc_ref[c, width:width + 1, :]) for c in range(2)]
    o_t = outs[0] - lam * outs[1]
    ms = jnp.mean(o_t * o_t, axis=0, keepdims=True)
    o_t = o_t * lax.rsqrt(ms + SUBLN_EPS) * scale
    o_ref[rows, :] = o_t.T.astype(BF16)


def _diff_attention(q, k, vt, lq1, lk1, lq2, lk2, subln, lam_init, tq):
    bsz, seq, d = q.shape
    width = 2 * HEAD_DIM
    vec = _resident((1, HEAD_DIM))
    head_spec = pl.BlockSpec((None, seq, width), lambda b, h: (b, 0, h))
    vt_spec = pl.BlockSpec((None, seq // tq, width, tq), lambda b, h: (b, 0, h, 0))
    return pl.pallas_call(
        functools.partial(_diff_kernel, tq=tq, lam_init=lam_init),
        grid=(bsz, B_HEADS),
        in_specs=[vec, vec, vec, vec, _resident((width, 1)), head_spec, head_spec, vt_spec],
        out_specs=head_spec,
        out_shape=jax.ShapeDtypeStruct((bsz, seq, d), BF16),
        scratch_shapes=[pltpu.VMEM((2, width + ONES_ROWS, tq), F32), pltpu.VMEM((2, 1, tq), F32),
                        pltpu.VMEM((2, 2, tq, tq), F32),
                        pltpu.VMEM((2, 2, 1, tq), F32)],
        compiler_params=_params(2),
        name="diff_attn",
    )(lq1.reshape(1, HEAD_DIM), lk1.reshape(1, HEAD_DIM), lq2.reshape(1, HEAD_DIM),
      lk2.reshape(1, HEAD_DIM), subln.reshape(width, 1), q, k, vt)


def _split3(m):
    return jnp.split(m[:, None, :], 3, axis=-1)


def kernel(x, c, positions, mod_mix_w, mod_mix_b, mod_ffn_w, mod_ffn_b, norm_pre_mix, norm_post_mix, norm_pre_ffn, norm_post_ffn, ffn_w_gate, ffn_w_up, ffn_conv_w, ffn_conv_b, ffn_w_down, a_w_qkv, a_w_o, kv_norm, kv_mod_w, kv_mod_b, b_w_k, b_w_v, b_w_q, b_lambda_q1, b_lambda_k1, b_lambda_q2, b_lambda_k2, b_subln, b_w_o):
    mix_mod = _ada_mod(c, mod_mix_w, mod_mix_b)
    ffn_mod = _ada_mod(c, mod_ffn_w, mod_ffn_b)
    kv_mod = _ada_mod(c, kv_mod_w[None], kv_mod_b[None])[0]
    tables = _rope_tables(positions)

    def tail(attn, x, w_o, gate_mix, l):
        shift, scale, gate = _split3(ffn_mod[l])
        return _layer_tail(attn, x, w_o.astype(BF16), norm_post_mix[l], gate_mix,
                           norm_pre_ffn[l], scale, shift, gate, ffn_w_gate[l].astype(BF16),
                           ffn_w_up[l].astype(BF16), ffn_conv_w[l], ffn_conv_b[l],
                           (0.5 * ffn_w_down[l]).astype(BF16), norm_post_ffn[l])

    shift, scale, gate = _split3(mix_mod[0])
    qkv_groups = _qkv0(x, norm_pre_mix[0], scale, shift, a_w_qkv[0].astype(BF16), tables)
    outs, lses = [], []
    for qkv, (_, dilation), (tq, unroll) in zip(qkv_groups, A_GROUPS, DIL_TILES):
        o, lse = _dilated_attention(qkv, dilation, tq, unroll)
        outs.append(o)
        lses.append(lse)
    x = tail((outs, lses), x, a_w_o[0], gate, 0)

    shift, scale, gate = _split3(mix_mod[1])
    kv_shift, kv_scale = jnp.split(kv_mod[:, None, :], 2, axis=-1)
    q, k, vt = _qkv1(x, norm_pre_mix[1], scale, shift, kv_norm, kv_scale, kv_shift,
                     b_w_q[0].astype(BF16), b_w_k.astype(BF16), b_w_v.T.astype(BF16), tables,
                     tm=DIFF_TILE)
    lam_init = 0.8 - 0.6 * math.exp(-0.3 * 1)
    o = _diff_attention(q, k, vt, b_lambda_q1[0], b_lambda_k1[0], b_lambda_q2[0], b_lambda_k2[0],
                        b_subln[0], lam_init, tq=DIFF_TILE)
    x = tail(o, x, b_w_o[0], gate, 1)
    return x
```

```python
import functools
import math

import jax
import jax.numpy as jnp
from jax import lax
from jax.experimental import pallas as pl
from jax.experimental.pallas import tpu as pltpu

D_MODEL = 1024
HEAD_DIM = 64
ROT_DIM = HEAD_DIM // 4
ROT_HALF = ROT_DIM // 2
ROPE_THETA = 500000.0
A_GROUPS = ((128, 1), (512, 4), (2048, 16))
A_HEADS = 8
A_INNER = A_HEADS * HEAD_DIM
B_HEADS = D_MODEL // (2 * HEAD_DIM)
CONV_WIDTH = 3
NORM_EPS = 1e-6
SUBLN_EPS = 1e-5
NEG_INF = -1e30
LOG2E = math.log2(math.e)
LN2 = math.log(2.0)
Q_SCALE = HEAD_DIM ** -0.5 * LOG2E
GELU_C1 = math.sqrt(2.0 / math.pi)
GELU_C3 = GELU_C1 * 0.044715

LANES = 128
SUBLANES = 8
WIN = 128
PROJ_CHUNK = 512
VT_CHUNK = 256
FFN_CHUNK = 256
DIFF_TILE = 512
ONES_ROWS = 16
DIL_TILES = ((512, 1), (256, 4), (128, 8))
VMEM_LIMIT = 56 * 1024 * 1024

F32 = jnp.float32
BF16 = jnp.bfloat16


def _resident(shape):
    zeros = (0,) * len(shape)
    return pl.BlockSpec(shape, lambda *_: zeros, pipeline_mode=pl.Buffered(1))


def _params(n_axes, sequential=False):
    sem = ("arbitrary",) * n_axes if sequential else ("parallel",) * n_axes
    return pltpu.CompilerParams(dimension_semantics=sem, vmem_limit_bytes=VMEM_LIMIT)


def _normed(x, eps):
    return x * lax.rsqrt(jnp.mean(x * x, axis=-1, keepdims=True) + eps)


def _rope128(y, c, sa, sb):
    return y * c + pltpu.roll(y, LANES - ROT_HALF, 1) * sa + pltpu.roll(y, ROT_HALF, 1) * sb


def _mod_kernel(c_ref, w_ref, b_ref, o_ref):
    c = c_ref[...]
    c_act = c / (1.0 + jnp.exp(-c))
    o_ref[...] = jnp.dot(c_act, w_ref[...], preferred_element_type=F32) + b_ref[...]


def _ada_mod(c, w, b, tn=1024):
    n_layers, d, n = w.shape
    bsz = c.shape[0]
    return pl.pallas_call(
        _mod_kernel,
        grid=(n_layers, n // tn),
        in_specs=[
            pl.BlockSpec((bsz, d), lambda l, j: (0, 0)),
            pl.BlockSpec((None, d, tn), lambda l, j: (l, 0, j)),
            pl.BlockSpec((None, 1, tn), lambda l, j: (l, 0, j)),
        ],
        out_specs=pl.BlockSpec((None, bsz, tn), lambda l, j: (l, 0, j)),
        out_shape=jax.ShapeDtypeStruct((n_layers, bsz, n), F32),
        compiler_params=_params(2),
        name="ada_mod",
    )(c, w, b.reshape(n_layers, 1, n))


def _rope_kernel(pos_ref, inv_ref, c_ref, sa_ref, sb_ref):
    ang = pos_ref[...].astype(F32) * inv_ref[...]
    cos = jnp.cos(ang)
    sin = jnp.sin(ang)
    lane = lax.broadcasted_iota(jnp.int32, ang.shape, 1) & (HEAD_DIM - 1)
    c_ref[...] = cos
    sa_ref[...] = jnp.where(lane < ROT_HALF, -sin, 0.0)
    sb_ref[...] = jnp.where((lane >= ROT_HALF) & (lane < ROT_DIM), sin, 0.0)


def _rope_tables(positions, ts=1024):
    bsz, seq = positions.shape
    inv = jnp.power(ROPE_THETA, -jnp.arange(0, ROT_DIM, 2, dtype=F32) / ROT_DIM)
    head = jnp.concatenate([inv, inv, jnp.zeros((HEAD_DIM - ROT_DIM,), F32)])
    inv_lane = jnp.tile(head, LANES // HEAD_DIM).reshape(1, LANES)
    spec = pl.BlockSpec((None, ts, LANES), lambda b, i: (b, i, 0))
    shape = jax.ShapeDtypeStruct((bsz, seq, LANES), F32)
    return pl.pallas_call(
        _rope_kernel,
        grid=(bsz, seq // ts),
        in_specs=[pl.BlockSpec((None, ts, 1), lambda b, i: (b, i, 0)),
                  pl.BlockSpec((1, LANES), lambda b, i: (0, 0))],
        out_specs=[spec, spec, spec],
        out_shape=[shape, shape, shape],
        compiler_params=_params(2),
        name="rope_tables",
    )(positions.reshape(bsz, seq, 1), inv_lane)


def _qkv0_kernel(x_ref, g_ref, sc_ref, sh_ref, w_ref, c_ref, sa_ref, sb_ref, *rest, tm):
    out_refs, stage_ref = rest[:-1], rest[-1]
    h = _normed(x_ref[...], NORM_EPS) * (g_ref[...] * (1.0 + sc_ref[...])) + sh_ref[...]
    hb = h.astype(BF16)
    c, sa, sb = c_ref[...], sa_ref[...], sb_ref[...]
    slabs = A_INNER // LANES
    stage = 0
    for grp, (_, dil) in reversed(list(enumerate(A_GROUPS))):
        o_ref = out_refs[grp]
        for kind in range(3):
            lo = (grp * 3 + kind) * A_INNER
            y = jnp.dot(hb, w_ref[:, lo:lo + A_INNER], preferred_element_type=F32)
            for j in range(slabs):
                yj = y[:, j * LANES:(j + 1) * LANES]
                if kind < 2:
                    yj = _rope128(yj, c, sa, sb)
                if kind == 0:
                    yj = yj * Q_SCALE
                cols = slice(kind * A_INNER + j * LANES, kind * A_INNER + (j + 1) * LANES)
                if dil == 1:
                    o_ref[0, :, cols] = yj.astype(BF16)
                    continue
                stage_ref[stage] = yj
                for r in range(dil):
                    rows = stage_ref[stage, pl.ds(r, tm // dil, stride=dil), :]
                    o_ref[r, :, cols] = rows.astype(BF16)
                stage += 1


def _qkv0(x, gain, scale, shift, w, tables, tm=1024):
    bsz, seq, d = x.shape
    n = w.shape[1]
    row = pl.BlockSpec((None, 1, d), lambda b, i: (b, 0, 0))
    tab = pl.BlockSpec((None, tm, LANES), lambda b, i: (b, i, 0))
    width = 3 * A_INNER
    dils = [dil for _, dil in A_GROUPS]
    n_stage = sum(3 * (A_INNER // LANES) for dil in dils if dil > 1)
    return pl.pallas_call(
        functools.partial(_qkv0_kernel, tm=tm),
        grid=(bsz, seq // tm),
        in_specs=[pl.BlockSpec((None, tm, d), lambda b, i: (b, i, 0)),
                  _resident((1, d)), row, row, _resident((d, n)), tab, tab, tab],
        out_specs=[pl.BlockSpec((None, dil, tm // dil, width), lambda b, i: (b, 0, i, 0)) for dil in dils],
        out_shape=[jax.ShapeDtypeStruct((bsz, dil, seq // dil, width), BF16) for dil in dils],
        scratch_shapes=[pltpu.VMEM((n_stage, tm, LANES), F32)],
        compiler_params=_params(2),
        name="qkv0",
    )(x, gain.reshape(1, d), scale, shift, w, *tables)


def _dil_kernel(q_ref, kp_ref, k_ref, vp_ref, v_ref, o_ref, lse_ref, *, tq, dil, unroll):
    first_tile = pl.program_id(1) == 0
    row = lax.broadcasted_iota(jnp.int32, (2 * WIN, 2 * WIN), 0) & (WIN - 1)
    col = lax.broadcasted_iota(jnp.int32, (2 * WIN, 2 * WIN), 1)
    band = (col >= row) & (col <= row + WIN)
    band_first = band & ((col >= WIN) | jnp.logical_not(first_tile))
    lane = lax.broadcasted_iota(jnp.int32, (WIN, LANES), 1)
    low_q = lane < HEAD_DIM
    low_kv = lax.broadcasted_iota(jnp.int32, (2 * WIN, LANES), 1) < HEAD_DIM
    first_head = lax.broadcasted_iota(jnp.int32, (4 * WIN, LANES), 0) < 2 * WIN
    first_lanes = lax.broadcasted_iota(jnp.int32, (4 * WIN, LANES), 1) < HEAD_DIM
    ones2 = jnp.where(first_head == first_lanes, 1.0, 0.0).astype(BF16)

    def residue(r):
        for j in range(tq // WIN):
            rows = slice(j * WIN, (j + 1) * WIN)
            if dil == 1:
                out_rows = rows
            else:
                out_rows = pl.ds(j * WIN * dil + r, WIN, stride=dil)
            for hp in range(A_HEADS // 2):
                cols = slice(hp * LANES, (hp + 1) * LANES)
                q = q_ref[r, rows, cols]
                if j == 0:
                    kk = jnp.concatenate([kp_ref[r, :, cols], k_ref[r, 0:WIN, cols]], axis=0)
                    vv = jnp.concatenate([vp_ref[r, :, cols], v_ref[r, 0:WIN, cols]], axis=0)
                else:
                    kk = k_ref[r, (j - 1) * WIN:(j + 1) * WIN, cols]
                    vv = v_ref[r, (j - 1) * WIN:(j + 1) * WIN, cols]
                zq, zv = jnp.zeros_like(q), jnp.zeros_like(vv)
                q2 = jnp.concatenate([jnp.where(low_q, q, zq), jnp.where(low_q, zq, q)], axis=0)
                s = lax.dot_general(q2, kk, (((1,), (1,)), ((), ())), preferred_element_type=F32)
                s = jnp.where(band_first if j == 0 else band, s, NEG_INF)
                mx = jnp.max(s, axis=-1, keepdims=True)
                pb = jnp.exp2(s - mx).astype(BF16)
                p2 = jnp.concatenate([pb[0:WIN], pb[WIN:2 * WIN]], axis=1)
                v2 = jnp.concatenate([jnp.where(low_kv, vv, zv), jnp.where(low_kv, zv, vv)], axis=0)
                od = jnp.dot(p2, jnp.concatenate([v2, ones2], axis=1), preferred_element_type=F32)
                den = od[:, LANES:]
                o_ref[hp, out_rows, :] = od[:, :LANES] * (1.0 / den)
                mx_lanes = jnp.where(low_q, mx[0:WIN], mx[WIN:2 * WIN])
                lse_ref[hp, out_rows, :] = mx_lanes * LN2 + jnp.log(den)

    if dil == 1:
        residue(0)
    else:
        def body(t, carry):
            for u in range(unroll):
                residue(t * unroll + u)
            return carry
        lax.fori_loop(0, dil // unroll, body, 0)


def _dilated_attention(qkv, dilation, tq, unroll=1):
    bsz, _, length, width = qkv.shape
    seq = length * dilation
    sub = tq // WIN
    span = tq * dilation

    def cur(kind):
        return pl.BlockSpec((None, dilation, tq, A_INNER), lambda b, i: (b, 0, i, kind))

    def prev(kind):
        return pl.BlockSpec((None, dilation, WIN, A_INNER),
                            lambda b, i: (b, 0, jnp.maximum(i * sub - 1, 0), kind))

    return pl.pallas_call(
        functools.partial(_dil_kernel, tq=tq, dil=dilation, unroll=unroll),
        grid=(bsz, length // tq),
        in_specs=[cur(0), prev(1), cur(1), prev(2), cur(2)],
        out_specs=[pl.BlockSpec((None, A_INNER // LANES, span, LANES), lambda b, i: (b, 0, i, 0))] * 2,
        out_shape=[jax.ShapeDtypeStruct((bsz, A_INNER // LANES, seq, LANES), F32)] * 2,
        compiler_params=_params(2),
        name=f"dilated_attn_d{dilation}",
    )(qkv, qkv, qkv, qkv, qkv)


def _mixture(o_refs, l_refs):
    slabs = []
    for hp in range(A_HEADS // 2):
        lses = [l_ref[hp] for l_ref in l_refs]
        m = functools.reduce(jnp.maximum, lses)
        es = [jnp.exp(l - m) for l in lses]
        inv = 1.0 / functools.reduce(lambda a, b: a + b, es)
        mixed = functools.reduce(lambda a, b: a + b, [e * o_ref[hp] for e, o_ref in zip(es, o_refs)])
        slabs.append((mixed * inv).astype(BF16))
    return jnp.concatenate(slabs, axis=1)


def _tail_kernel(*refs, tm, fc, n_groups):
    n_attn = 2 * n_groups if n_groups else 1
    attn_refs = refs[:n_attn]
    (x_ref, wo_ref, gmix_ref, gatemix_ref, gpre_ref, sc_ref, sh_ref, gate_ref, wg_ref, wu_ref,
     cw_ref, cb_ref, wd_ref, gpost_ref, out_ref, a_buf, act_buf) = refs[n_attn:]
    n_chunks = a_buf.shape[0]

    @pl.when(pl.program_id(1) == 0)
    def _():
        for ch in range(n_chunks):
            a_buf[ch, 0:SUBLANES, :] = jnp.zeros((SUBLANES, fc), F32)

    if n_groups:
        attn = _mixture(attn_refs[:n_groups], attn_refs[n_groups:])
    else:
        attn = attn_refs[0][...]
    y_attn = jnp.dot(attn, wo_ref[...], preferred_element_type=F32)
    x = x_ref[...] + _normed(y_attn, NORM_EPS) * (gatemix_ref[...] * gmix_ref[...])
    h = _normed(x, NORM_EPS) * (gpre_ref[...] * (1.0 + sc_ref[...])) + sh_ref[...]
    hb = h.astype(BF16)

    def gate_up(ch):
        cols = slice(ch * fc, (ch + 1) * fc)
        a_buf[ch, SUBLANES:SUBLANES + tm, :] = jnp.dot(hb, wg_ref[:, cols], preferred_element_type=F32)
        return jnp.dot(hb, wu_ref[:, cols], preferred_element_type=F32)

    u_next = gate_up(0)
    for ch in range(n_chunks):
        cols = slice(ch * fc, (ch + 1) * fc)
        u = u_next
        if ch + 1 < n_chunks:
            u_next = gate_up(ch + 1)
        conv = cb_ref[:, cols]
        for t in range(CONV_WIDTH):
            start = SUBLANES - (CONV_WIDTH - 1) + t
            conv = conv + cw_ref[t:t + 1, cols] * a_buf[ch, start:start + tm, :]
        inner = conv * (GELU_C1 + GELU_C3 * (conv * conv))
        act = conv * (1.0 + jnp.tanh(inner)) * u
        act_buf[:, cols] = act.astype(BF16)
        a_buf[ch, 0:SUBLANES, :] = a_buf[ch, tm:tm + SUBLANES, :]
    y = jnp.dot(act_buf[...], wd_ref[...], preferred_element_type=F32)
    out_ref[...] = x + _normed(y, NORM_EPS) * (gate_ref[...] * gpost_ref[...])


def _layer_tail(attn, x, wo, gmix, gate_mix, gpre, scale, shift, gate, wg, wu, cw, cb, wd, gpost,
                tm=512, fc=FFN_CHUNK):
    bsz, seq, d = x.shape
    d_ff = wg.shape[1]
    n_chunks = d_ff // fc
    x_spec = pl.BlockSpec((None, tm, d), lambda b, i: (b, i, 0))
    row = pl.BlockSpec((None, 1, d), lambda b, i: (b, 0, 0))
    if isinstance(attn, tuple):
        outs, lses = attn
        n_groups = len(outs)
        attn_args = [*outs, *lses]
        attn_specs = [pl.BlockSpec((None, A_INNER // LANES, tm, LANES),
                                   lambda b, i: (b, 0, i, 0))] * (2 * n_groups)
    else:
        n_groups = 0
        attn_args = [attn]
        attn_specs = [x_spec]
    return pl.pallas_call(
        functools.partial(_tail_kernel, tm=tm, fc=fc, n_groups=n_groups),
        grid=(bsz, seq // tm),
        in_specs=attn_specs + [
            x_spec, _resident(wo.shape), _resident((1, d)), row,
            _resident((1, d)), row, row, row,
            _resident((d, d_ff)), _resident((d, d_ff)), _resident((CONV_WIDTH, d_ff)),
            _resident((1, d_ff)), _resident((d_ff, d)), _resident((1, d))],
        out_specs=x_spec,
        out_shape=jax.ShapeDtypeStruct((bsz, seq, d), F32),
        scratch_shapes=[pltpu.VMEM((n_chunks, tm + SUBLANES, fc), F32), pltpu.VMEM((tm, d_ff), BF16)],
        compiler_params=_params(2, sequential=True),
        name="layer_tail",
    )(*attn_args, x, wo, gmix.reshape(1, d), gate_mix, gpre.reshape(1, d), scale, shift, gate,
      wg, wu, cw, cb.reshape(1, d_ff), wd, gpost.reshape(1, d))


def _qkv1_kernel(x_ref, gq_ref, scq_ref, shq_ref, gkv_ref, sckv_ref, shkv_ref, wq_ref, wk_ref, wvt_ref,
                 c_ref, sa_ref, sb_ref, q_ref, k_ref, vt_ref):
    x = x_ref[...]
    xn = _normed(x, NORM_EPS)
    hq = (xn * (gq_ref[...] * (1.0 + scq_ref[...])) + shq_ref[...]).astype(BF16)
    hkv = (xn * (gkv_ref[...] * (1.0 + sckv_ref[...])) + shkv_ref[...]).astype(BF16)
    c, sa, sb = c_ref[...], sa_ref[...], sb_ref[...]
    d = x.shape[1]
    for lo in range(0, d, PROJ_CHUNK):
        yq = jnp.dot(hq, wq_ref[:, lo:lo + PROJ_CHUNK], preferred_element_type=F32)
        yk = jnp.dot(hkv, wk_ref[:, lo:lo + PROJ_CHUNK], preferred_element_type=F32)
        for j in range(PROJ_CHUNK // LANES):
            cols = slice(lo + j * LANES, lo + (j + 1) * LANES)
            sub = slice(j * LANES, (j + 1) * LANES)
            q_ref[:, cols] = (_rope128(yq[:, sub], c, sa, sb) * Q_SCALE).astype(BF16)
            k_ref[:, cols] = _rope128(yk[:, sub], c, sa, sb).astype(BF16)
    for lo in range(0, d, VT_CHUNK):
        vt = lax.dot_general(wvt_ref[lo:lo + VT_CHUNK, :], hkv, (((1,), (1,)), ((), ())),
                             preferred_element_type=F32)
        vt_ref[lo:lo + VT_CHUNK, :] = vt.astype(BF16)


def _qkv1(x, gq, scq, shq, gkv, sckv, shkv, wq, wk, wvt, tables, tm):
    bsz, seq, d = x.shape
    x_spec = pl.BlockSpec((None, tm, d), lambda b, i: (b, i, 0))
    row = pl.BlockSpec((None, 1, d), lambda b, i: (b, 0, 0))
    tab = pl.BlockSpec((None, tm, LANES), lambda b, i: (b, i, 0))
    shape = jax.ShapeDtypeStruct((bsz, seq, d), BF16)
    return pl.pallas_call(
        _qkv1_kernel,
        grid=(bsz, seq // tm),
        in_specs=[x_spec, _resident((1, d)), row, row, _resident((1, d)), row, row,
                  _resident((d, d)), _resident((d, d)), _resident((d, d)), tab, tab, tab],
        out_specs=[x_spec, x_spec, pl.BlockSpec((None, None, d, tm), lambda b, i: (b, i, 0, 0))],
        out_shape=[shape, shape, jax.ShapeDtypeStruct((bsz, seq // tm, d, tm), BF16)],
        compiler_params=_params(2),
        name="qkv1",
    )(x, gq.reshape(1, d), scq, shq, gkv.reshape(1, d), sckv, shkv, wq, wk, wvt, *tables)


def _diff_kernel(lq1_ref, lk1_ref, lq2_ref, lk2_ref, g_ref, q_ref, k_ref, vt_ref, o_ref,
                 *scratch, tq, lam_init):
    lam = (jnp.exp(jnp.sum(lq1_ref[...] * lk1_ref[...], axis=-1, keepdims=True))
           - jnp.exp(jnp.sum(lq2_ref[...] * lk2_ref[...], axis=-1, keepdims=True)) + lam_init)
    scale = g_ref[...] * (1.0 - lam_init)

    def tile(i, carry):
        _diff_tile(i, lam, scale, q_ref, k_ref, vt_ref, o_ref, *scratch, tq=tq)
        return carry

    lax.fori_loop(0, q_ref.shape[0] // tq, tile, 0)


def _diff_tile(i, lam, scale, q_ref, k_ref, vt_ref, o_ref, acc_ref, m_ref, s_ref, bmax_ref, *, tq):
    rows = pl.ds(pl.multiple_of(i * tq, tq), tq)
    q = q_ref[rows, :]
    lane = lax.broadcasted_iota(jnp.int32, q.shape, 1)
    zero = jnp.zeros_like(q)
    q_maps = (jnp.where(lane < HEAD_DIM, q, zero), jnp.where(lane >= HEAD_DIM, q, zero))
    acc_ref[...] = jnp.zeros(acc_ref.shape, F32)
    m_ref[...] = jnp.full(m_ref.shape, NEG_INF, F32)

    half = tq // 2
    full = (0, tq, 0, tq)
    diag_a = (0, half, 0, tq)
    diag_b = (half, half, half, half)

    def scores(jb, slot, part, masked):
        k0, nk, q0, nq = part
        start = jb * tq + k0
        if not isinstance(jb, int):
            start = pl.multiple_of(start, half)
        k = k_ref[pl.ds(start, nk), :]
        for c in range(2):
            s = lax.dot_general(k, q_maps[c][q0:q0 + nq, :], (((1,), (1,)), ((), ())),
                                preferred_element_type=F32)
            if masked:
                key = lax.broadcasted_iota(jnp.int32, (nk, nq), 0) + k0
                qry = lax.broadcasted_iota(jnp.int32, (nk, nq), 1) + q0
                s = jnp.where(key <= qry, s, NEG_INF)
            s_ref[slot, c, 0:nk, q0:q0 + nq] = s
            bmax_ref[slot, c, :, q0:q0 + nq] = jnp.max(s, axis=0, keepdims=True)

    def update(jb, slot, part):
        k0, nk, q0, nq = part
        lanes = slice(q0, q0 + nq)
        vt = jnp.concatenate([vt_ref[jb, :, k0:k0 + nk], jnp.ones((ONES_ROWS, nk), BF16)], axis=0)
        for c in range(2):
            s = s_ref[slot, c, 0:nk, lanes]
            m_old = m_ref[c, :, lanes]
            m_new = jnp.maximum(m_old, bmax_ref[slot, c, :, lanes])
            alpha = jnp.exp2(m_old - m_new)
            p = jnp.exp2(s - m_new)
            acc_ref[c, :, lanes] = (alpha * acc_ref[c, :, lanes]
                                    + jnp.dot(vt, p.astype(BF16), preferred_element_type=F32))
            m_ref[c, :, lanes] = m_new

    def diagonal(prev_slot):
        a_slot = 0 if prev_slot is None else 1 - prev_slot
        scores(i, a_slot, diag_a, True)
        if prev_slot is not None:
            update(i - 1, prev_slot, full)
        scores(i, 1 - a_slot, diag_b, True)
        update(i, a_slot, diag_a)
        update(i, 1 - a_slot, diag_b)

    odd = (i & 1) == 1

    @pl.when(i == 0)
    def _():
        diagonal(None)

    @pl.when(i > 0)
    def _():
        scores(0, 0, full, False)

    def body(t, carry):
        for slot in range(2):
            scores(2 * t + slot + 1, 1 - slot, full, False)
            update(2 * t + slot, slot, full)
        return carry

    lax.fori_loop(0, lax.shift_right_arithmetic(i - 1, 1), body, 0)

    @pl.when(odd)
    def _():
        diagonal(0)

    @pl.when(jnp.logical_and(i > 0, jnp.logical_not(odd)))
    def _():
        scores(i - 1, 1, full, False)
        update(i - 2, 0, full)
        diagonal(1)

    width = 2 * HEAD_DIM
    outs = [acc_ref[c, 0:width, :] * (1.0 / acc_ref[c, width:width + 1, :]) for c in range(2)]
    o_t = outs[0] - lam * outs[1]
    ms = jnp.mean(o_t * o_t, axis=0, keepdims=True)
    o_t = o_t * lax.rsqrt(ms + SUBLN_EPS) * scale
    o_ref[rows, :] = o_t.astype(BF16).T


def _diff_attention(q, k, vt, lq1, lk1, lq2, lk2, subln, lam_init, tq):
    bsz, seq, d = q.shape
    width = 2 * HEAD_DIM
    vec = _resident((1, HEAD_DIM))
    head_spec = pl.BlockSpec((None, seq, width), lambda b, h: (b, 0, h))
    vt_spec = pl.BlockSpec((None, seq // tq, width, tq), lambda b, h: (b, 0, h, 0))
    return pl.pallas_call(
        functools.partial(_diff_kernel, tq=tq, lam_init=lam_init),
        grid=(bsz, B_HEADS),
        in_specs=[vec, vec, vec, vec, _resident((width, 1)), head_spec, head_spec, vt_spec],
        out_specs=head_spec,
        out_shape=jax.ShapeDtypeStruct((bsz, seq, d), BF16),
        scratch_shapes=[pltpu.VMEM((2, width + ONES_ROWS, tq), F32), pltpu.VMEM((2, 1, tq), F32),
                        pltpu.VMEM((2, 2, tq, tq), F32),
                        pltpu.VMEM((2, 2, 1, tq), F32)],
        compiler_params=_params(2),
        name="diff_attn",
    )(lq1.reshape(1, HEAD_DIM), lk1.reshape(1, HEAD_DIM), lq2.reshape(1, HEAD_DIM),
      lk2.reshape(1, HEAD_DIM), subln.reshape(width, 1), q, k, vt)


def _split3(m):
    return jnp.split(m[:, None, :], 3, axis=-1)


def kernel(x, c, positions, mod_mix_w, mod_mix_b, mod_ffn_w, mod_ffn_b, norm_pre_mix, norm_post_mix, norm_pre_ffn, norm_post_ffn, ffn_w_gate, ffn_w_up, ffn_conv_w, ffn_conv_b, ffn_w_down, a_w_qkv, a_w_o, kv_norm, kv_mod_w, kv_mod_b, b_w_k, b_w_v, b_w_q, b_lambda_q1, b_lambda_k1, b_lambda_q2, b_lambda_k2, b_subln, b_w_o):
    mix_mod = _ada_mod(c, mod_mix_w, mod_mix_b)
    ffn_mod = _ada_mod(c, mod_ffn_w, mod_ffn_b)
    kv_mod = _ada_mod(c, kv_mod_w[None], kv_mod_b[None])[0]
    tables = _rope_tables(positions)

    def tail(attn, x, w_o, gate_mix, l):
        shift, scale, gate = _split3(ffn_mod[l])
        return _layer_tail(attn, x, w_o.astype(BF16), norm_post_mix[l], gate_mix,
                           norm_pre_ffn[l], scale, shift, gate, ffn_w_gate[l].astype(BF16),
                           ffn_w_up[l].astype(BF16), ffn_conv_w[l], ffn_conv_b[l],
                           (0.5 * ffn_w_down[l]).astype(BF16), norm_post_ffn[l])

    shift, scale, gate = _split3(mix_mod[0])
    qkv_groups = _qkv0(x, norm_pre_mix[0], scale, shift, a_w_qkv[0].astype(BF16), tables)
    outs, lses = [], []
    for qkv, (_, dilation), (tq, unroll) in zip(qkv_groups, A_GROUPS, DIL_TILES):
        o, lse = _dilated_attention(qkv, dilation, tq, unroll)
        outs.append(o)
        lses.append(lse)
    x = tail((outs, lses), x, a_w_o[0], gate, 0)

    shift, scale, gate = _split3(mix_mod[1])
    kv_shift, kv_scale = jnp.split(kv_mod[:, None, :], 2, axis=-1)
    q, k, vt = _qkv1(x, norm_pre_mix[1], scale, shift, kv_norm, kv_scale, kv_shift,
                     b_w_q[0].astype(BF16), b_w_k.astype(BF16), b_w_v.T.astype(BF16), tables,
                     tm=DIFF_TILE)
    lam_init = 0.8 - 0.6 * math.exp(-0.3 * 1)
    o = _diff_attention(q, k, vt, b_lambda_q1[0], b_lambda_k1[0], b_lambda_q2[0], b_lambda_k2[0],
                        b_subln[0], lam_init, tq=DIFF_TILE)
    x = tail(o, x, b_w_o[0], gate, 1)
    return x
```

```python
import functools
import math

import jax
import jax.numpy as jnp
from jax import lax
from jax.experimental import pallas as pl
from jax.experimental.pallas import tpu as pltpu

D_MODEL = 1024
HEAD_DIM = 64
ROT_DIM = HEAD_DIM // 4
ROT_HALF = ROT_DIM // 2
ROPE_THETA = 500000.0
A_GROUPS = ((128, 1), (512, 4), (2048, 16))
A_HEADS = 8
A_INNER = A_HEADS * HEAD_DIM
B_HEADS = D_MODEL // (2 * HEAD_DIM)
CONV_WIDTH = 3
NORM_EPS = 1e-6
SUBLN_EPS = 1e-5
NEG_INF = -1e30
LOG2E = math.log2(math.e)
LN2 = math.log(2.0)
Q_SCALE = HEAD_DIM ** -0.5 * LOG2E
GELU_C1 = math.sqrt(2.0 / math.pi)
GELU_C3 = GELU_C1 * 0.044715

LANES = 128
SUBLANES = 8
WIN = 128
PROJ_CHUNK = 512
VT_CHUNK = 256
FFN_CHUNK = 256
DIFF_TILE = 512
ONES_ROWS = 16
DIL_TILES = ((1024, 1), (256, 4), (128, 8))
VMEM_LIMIT = 56 * 1024 * 1024

F32 = jnp.float32
BF16 = jnp.bfloat16


def _resident(shape):
    zeros = (0,) * len(shape)
    return pl.BlockSpec(shape, lambda *_: zeros, pipeline_mode=pl.Buffered(1))


def _params(n_axes, sequential=False):
    sem = ("arbitrary",) * n_axes if sequential else ("parallel",) * n_axes
    return pltpu.CompilerParams(dimension_semantics=sem, vmem_limit_bytes=VMEM_LIMIT)


def _normed(x, eps):
    return x * lax.rsqrt(jnp.mean(x * x, axis=-1, keepdims=True) + eps)


def _rope128(y, c, sa, sb):
    return y * c + pltpu.roll(y, LANES - ROT_HALF, 1) * sa + pltpu.roll(y, ROT_HALF, 1) * sb


def _mod_kernel(c_ref, w_ref, b_ref, o_ref):
    c = c_ref[...]
    c_act = c / (1.0 + jnp.exp(-c))
    o_ref[...] = jnp.dot(c_act, w_ref[...], preferred_element_type=F32) + b_ref[...]


def _ada_mod(c, w, b, tn=1024):
    n_layers, d, n = w.shape
    bsz = c.shape[0]
    return pl.pallas_call(
        _mod_kernel,
        grid=(n_layers, n // tn),
        in_specs=[
            pl.BlockSpec((bsz, d), lambda l, j: (0, 0)),
            pl.BlockSpec((None, d, tn), lambda l, j: (l, 0, j)),
            pl.BlockSpec((None, 1, tn), lambda l, j: (l, 0, j)),
        ],
        out_specs=pl.BlockSpec((None, bsz, tn), lambda l, j: (l, 0, j)),
        out_shape=jax.ShapeDtypeStruct((n_layers, bsz, n), F32),
        compiler_params=_params(2),
        name="ada_mod",
    )(c, w, b.reshape(n_layers, 1, n))


def _rope_kernel(pos_ref, inv_ref, c_ref, sa_ref, sb_ref):
    ang = pos_ref[...].astype(F32) * inv_ref[...]
    cos = jnp.cos(ang)
    sin = jnp.sin(ang)
    lane = lax.broadcasted_iota(jnp.int32, ang.shape, 1) & (HEAD_DIM - 1)
    c_ref[...] = cos
    sa_ref[...] = jnp.where(lane < ROT_HALF, -sin, 0.0)
    sb_ref[...] = jnp.where((lane >= ROT_HALF) & (lane < ROT_DIM), sin, 0.0)


def _rope_tables(positions, ts=1024):
    bsz, seq = positions.shape
    inv = jnp.power(ROPE_THETA, -jnp.arange(0, ROT_DIM, 2, dtype=F32) / ROT_DIM)
    head = jnp.concatenate([inv, inv, jnp.zeros((HEAD_DIM - ROT_DIM,), F32)])
    inv_lane = jnp.tile(head, LANES // HEAD_DIM).reshape(1, LANES)
    spec = pl.BlockSpec((None, ts, LANES), lambda b, i: (b, i, 0))
    shape = jax.ShapeDtypeStruct((bsz, seq, LANES), F32)
    return pl.pallas_call(
        _rope_kernel,
        grid=(bsz, seq // ts),
        in_specs=[pl.BlockSpec((None, ts, 1), lambda b, i: (b, i, 0)),
                  pl.BlockSpec((1, LANES), lambda b, i: (0, 0))],
        out_specs=[spec, spec, spec],
        out_shape=[shape, shape, shape],
        compiler_params=_params(2),
        name="rope_tables",
    )(positions.reshape(bsz, seq, 1), inv_lane)


def _qkv0_kernel(x_ref, g_ref, sc_ref, sh_ref, w_ref, c_ref, sa_ref, sb_ref, *rest, tm):
    out_refs, stage_ref = rest[:-1], rest[-1]
    h = _normed(x_ref[...], NORM_EPS) * (g_ref[...] * (1.0 + sc_ref[...])) + sh_ref[...]
    hb = h.astype(BF16)
    c, sa, sb = c_ref[...], sa_ref[...], sb_ref[...]
    slabs = A_INNER // LANES
    stage = 0
    for grp, (_, dil) in reversed(list(enumerate(A_GROUPS))):
        o_ref = out_refs[grp]
        for kind in range(3):
            lo = (grp * 3 + kind) * A_INNER
            y = jnp.dot(hb, w_ref[:, lo:lo + A_INNER], preferred_element_type=F32)
            for j in range(slabs):
                yj = y[:, j * LANES:(j + 1) * LANES]
                if kind < 2:
                    yj = _rope128(yj, c, sa, sb)
                if kind == 0:
                    yj = yj * Q_SCALE
                cols = slice(kind * A_INNER + j * LANES, kind * A_INNER + (j + 1) * LANES)
                if dil == 1:
                    o_ref[0, :, cols] = yj.astype(BF16)
                    continue
                stage_ref[stage] = yj
                for r in range(dil):
                    rows = stage_ref[stage, pl.ds(r, tm // dil, stride=dil), :]
                    o_ref[r, :, cols] = rows.astype(BF16)
                stage += 1


def _qkv0(x, gain, scale, shift, w, tables, tm=1024):
    bsz, seq, d = x.shape
    n = w.shape[1]
    row = pl.BlockSpec((None, 1, d), lambda b, i: (b, 0, 0))
    tab = pl.BlockSpec((None, tm, LANES), lambda b, i: (b, i, 0))
    width = 3 * A_INNER
    dils = [dil for _, dil in A_GROUPS]
    n_stage = sum(3 * (A_INNER // LANES) for dil in dils if dil > 1)
    return pl.pallas_call(
        functools.partial(_qkv0_kernel, tm=tm),
        grid=(bsz, seq // tm),
        in_specs=[pl.BlockSpec((None, tm, d), lambda b, i: (b, i, 0)),
                  _resident((1, d)), row, row, _resident((d, n)), tab, tab, tab],
        out_specs=[pl.BlockSpec((None, dil, tm // dil, width), lambda b, i: (b, 0, i, 0)) for dil in dils],
        out_shape=[jax.ShapeDtypeStruct((bsz, dil, seq // dil, width), BF16) for dil in dils],
        scratch_shapes=[pltpu.VMEM((n_stage, tm, LANES), F32)],
        compiler_params=_params(2),
        name="qkv0",
    )(x, gain.reshape(1, d), scale, shift, w, *tables)


def _dil_kernel(q_ref, kp_ref, k_ref, vp_ref, v_ref, o_ref, lse_ref, *, tq, dil, unroll):
    first_tile = pl.program_id(1) == 0
    row = lax.broadcasted_iota(jnp.int32, (2 * WIN, 2 * WIN), 0) & (WIN - 1)
    col = lax.broadcasted_iota(jnp.int32, (2 * WIN, 2 * WIN), 1)
    band = (col >= row) & (col <= row + WIN)
    band_first = band & ((col >= WIN) | jnp.logical_not(first_tile))
    lane = lax.broadcasted_iota(jnp.int32, (WIN, LANES), 1)
    low_q = lane < HEAD_DIM
    low_kv = lax.broadcasted_iota(jnp.int32, (2 * WIN, LANES), 1) < HEAD_DIM
    first_head = lax.broadcasted_iota(jnp.int32, (4 * WIN, LANES), 0) < 2 * WIN
    first_lanes = lax.broadcasted_iota(jnp.int32, (4 * WIN, LANES), 1) < HEAD_DIM
    ones2 = jnp.where(first_head == first_lanes, 1.0, 0.0).astype(BF16)

    def residue(r):
        for j in range(tq // WIN):
            rows = slice(j * WIN, (j + 1) * WIN)
            if dil == 1:
                out_rows = rows
            else:
                out_rows = pl.ds(j * WIN * dil + r, WIN, stride=dil)
            for hp in range(A_HEADS // 2):
                cols = slice(hp * LANES, (hp + 1) * LANES)
                q = q_ref[r, rows, cols]
                if j == 0:
                    kk = jnp.concatenate([kp_ref[r, :, cols], k_ref[r, 0:WIN, cols]], axis=0)
                    vv = jnp.concatenate([vp_ref[r, :, cols], v_ref[r, 0:WIN, cols]], axis=0)
                else:
                    kk = k_ref[r, (j - 1) * WIN:(j + 1) * WIN, cols]
                    vv = v_ref[r, (j - 1) * WIN:(j + 1) * WIN, cols]
                zq, zv = jnp.zeros_like(q), jnp.zeros_like(vv)
                q2 = jnp.concatenate([jnp.where(low_q, q, zq), jnp.where(low_q, zq, q)], axis=0)
                s = lax.dot_general(q2, kk, (((1,), (1,)), ((), ())), preferred_element_type=F32)
                s = jnp.where(band_first if j == 0 else band, s, NEG_INF)
                mx = jnp.max(s, axis=-1, keepdims=True)
                pb = jnp.exp2(s - mx).astype(BF16)
                p2 = jnp.concatenate([pb[0:WIN], pb[WIN:2 * WIN]], axis=1)
                v2 = jnp.concatenate([jnp.where(low_kv, vv, zv), jnp.where(low_kv, zv, vv)], axis=0)
                od = jnp.dot(p2, jnp.concatenate([v2, ones2], axis=1), preferred_element_type=F32)
                den = od[:, LANES:]
                o_ref[hp, out_rows, :] = od[:, :LANES] * (1.0 / den)
                mx_lanes = jnp.where(low_q, mx[0:WIN], mx[WIN:2 * WIN])
                lse_ref[hp, out_rows, :] = mx_lanes * LN2 + jnp.log(den)

    if dil == 1:
        residue(0)
    else:
        def body(t, carry):
            for u in range(unroll):
                residue(t * unroll + u)
            return carry
        lax.fori_loop(0, dil // unroll, body, 0)


def _dilated_attention(qkv, dilation, tq, unroll=1):
    bsz, _, length, width = qkv.shape
    seq = length * dilation
    sub = tq // WIN
    span = tq * dilation

    def cur(kind):
        return pl.BlockSpec((None, dilation, tq, A_INNER), lambda b, i: (b, 0, i, kind))

    def prev(kind):
        return pl.BlockSpec((None, dilation, WIN, A_INNER),
                            lambda b, i: (b, 0, jnp.maximum(i * sub - 1, 0), kind))

    return pl.pallas_call(
        functools.partial(_dil_kernel, tq=tq, dil=dilation, unroll=unroll),
        grid=(bsz, length // tq),
        in_specs=[cur(0), prev(1), cur(1), prev(2), cur(2)],
        out_specs=[pl.BlockSpec((None, A_INNER // LANES, span, LANES), lambda b, i: (b, 0, i, 0))] * 2,
        out_shape=[jax.ShapeDtypeStruct((bsz, A_INNER // LANES, seq, LANES), F32)] * 2,
        compiler_params=_params(2),
        name=f"dilated_attn_d{dilation}",
    )(qkv, qkv, qkv, qkv, qkv)


def _mixture(o_refs, l_refs):
    slabs = []
    for hp in range(A_HEADS // 2):
        lses = [l_ref[hp] for l_ref in l_refs]
        m = functools.reduce(jnp.maximum, lses)
        es = [jnp.exp(l - m) for l in lses]
        inv = 1.0 / functools.reduce(lambda a, b: a + b, es)
        mixed = functools.reduce(lambda a, b: a + b, [e * o_ref[hp] for e, o_ref in zip(es, o_refs)])
        slabs.append((mixed * inv).astype(BF16))
    return jnp.concatenate(slabs, axis=1)


def _tail_kernel(*refs, tm, fc, n_groups):
    n_attn = 2 * n_groups if n_groups else 1
    attn_refs = refs[:n_attn]
    (x_ref, wo_ref, gmix_ref, gatemix_ref, gpre_ref, sc_ref, sh_ref, gate_ref, wg_ref, wu_ref,
     cw_ref, cb_ref, wd_ref, gpost_ref, out_ref, a_buf, act_buf) = refs[n_attn:]
    n_chunks = a_buf.shape[0]

    @pl.when(pl.program_id(1) == 0)
    def _():
        for ch in range(n_chunks):
            a_buf[ch, 0:SUBLANES, :] = jnp.zeros((SUBLANES, fc), F32)

    if n_groups:
        attn = _mixture(attn_refs[:n_groups], attn_refs[n_groups:])
    else:
        attn = attn_refs[0][...]
    y_attn = jnp.dot(attn, wo_ref[...], preferred_element_type=F32)
    x = x_ref[...] + _normed(y_attn, NORM_EPS) * (gatemix_ref[...] * gmix_ref[...])
    h = _normed(x, NORM_EPS) * (gpre_ref[...] * (1.0 + sc_ref[...])) + sh_ref[...]
    hb = h.astype(BF16)

    def gate_up(ch):
        cols = slice(ch * fc, (ch + 1) * fc)
        a_buf[ch, SUBLANES:SUBLANES + tm, :] = jnp.dot(hb, wg_ref[:, cols], preferred_element_type=F32)
        return jnp.dot(hb, wu_ref[:, cols], preferred_element_type=F32)

    u_next = gate_up(0)
    for ch in range(n_chunks):
        cols = slice(ch * fc, (ch + 1) * fc)
        u = u_next
        if ch + 1 < n_chunks:
            u_next = gate_up(ch + 1)
        conv = cb_ref[:, cols]
        for t in range(CONV_WIDTH):
            start = SUBLANES - (CONV_WIDTH - 1) + t
            conv = conv + cw_ref[t:t + 1, cols] * a_buf[ch, start:start + tm, :]
        inner = conv * (GELU_C1 + GELU_C3 * (conv * conv))
        act = conv * (1.0 + jnp.tanh(inner)) * u
        act_buf[:, cols] = act.astype(BF16)
        a_buf[ch, 0:SUBLANES, :] = a_buf[ch, tm:tm + SUBLANES, :]
    y = jnp.dot(act_buf[...], wd_ref[...], preferred_element_type=F32)
    out_ref[...] = x + _normed(y, NORM_EPS) * (gate_ref[...] * gpost_ref[...])


def _layer_tail(attn, x, wo, gmix, gate_mix, gpre, scale, shift, gate, wg, wu, cw, cb, wd, gpost,
                tm=512, fc=FFN_CHUNK):
    bsz, seq, d = x.shape
    d_ff = wg.shape[1]
    n_chunks = d_ff // fc
    x_spec = pl.BlockSpec((None, tm, d), lambda b, i: (b, i, 0))
    row = pl.BlockSpec((None, 1, d), lambda b, i: (b, 0, 0))
    if isinstance(attn, tuple):
        outs, lses = attn
        n_groups = len(outs)
        attn_args = [*outs, *lses]
        attn_specs = [pl.BlockSpec((None, A_INNER // LANES, tm, LANES),
                                   lambda b, i: (b, 0, i, 0))] * (2 * n_groups)
    else:
        n_groups = 0
        attn_args = [attn]
        attn_specs = [x_spec]
    return pl.pallas_call(
        functools.partial(_tail_kernel, tm=tm, fc=fc, n_groups=n_groups),
        grid=(bsz, seq // tm),
        in_specs=attn_specs + [
            x_spec, _resident(wo.shape), _resident((1, d)), row,
            _resident((1, d)), row, row, row,
            _resident((d, d_ff)), _resident((d, d_ff)), _resident((CONV_WIDTH, d_ff)),
            _resident((1, d_ff)), _resident((d_ff, d)), _resident((1, d))],
        out_specs=x_spec,
        out_shape=jax.ShapeDtypeStruct((bsz, seq, d), F32),
        scratch_shapes=[pltpu.VMEM((n_chunks, tm + SUBLANES, fc), F32), pltpu.VMEM((tm, d_ff), BF16)],
        compiler_params=_params(2, sequential=True),
        name="layer_tail",
    )(*attn_args, x, wo, gmix.reshape(1, d), gate_mix, gpre.reshape(1, d), scale, shift, gate,
      wg, wu, cw, cb.reshape(1, d_ff), wd, gpost.reshape(1, d))


def _qkv1_kernel(x_ref, gq_ref, scq_ref, shq_ref, gkv_ref, sckv_ref, shkv_ref, wq_ref, wk_ref, wvt_ref,
                 c_ref, sa_ref, sb_ref, q_ref, k_ref, vt_ref):
    x = x_ref[...]
    xn = _normed(x, NORM_EPS)
    hq = (xn * (gq_ref[...] * (1.0 + scq_ref[...])) + shq_ref[...]).astype(BF16)
    hkv = (xn * (gkv_ref[...] * (1.0 + sckv_ref[...])) + shkv_ref[...]).astype(BF16)
    c, sa, sb = c_ref[...], sa_ref[...], sb_ref[...]
    d = x.shape[1]
    for lo in range(0, d, PROJ_CHUNK):
        yq = jnp.dot(hq, wq_ref[:, lo:lo + PROJ_CHUNK], preferred_element_type=F32)
        yk = jnp.dot(hkv, wk_ref[:, lo:lo + PROJ_CHUNK], preferred_element_type=F32)
        for j in range(PROJ_CHUNK // LANES):
            cols = slice(lo + j * LANES, lo + (j + 1) * LANES)
            sub = slice(j * LANES, (j + 1) * LANES)
            q_ref[:, cols] = (_rope128(yq[:, sub], c, sa, sb) * Q_SCALE).astype(BF16)
            k_ref[:, cols] = _rope128(yk[:, sub], c, sa, sb).astype(BF16)
    for lo in range(0, d, VT_CHUNK):
        vt = lax.dot_general(wvt_ref[lo:lo + VT_CHUNK, :], hkv, (((1,), (1,)), ((), ())),
                             preferred_element_type=F32)
        vt_ref[lo:lo + VT_CHUNK, :] = vt.astype(BF16)


def _qkv1(x, gq, scq, shq, gkv, sckv, shkv, wq, wk, wvt, tables, tm):
    bsz, seq, d = x.shape
    x_spec = pl.BlockSpec((None, tm, d), lambda b, i: (b, i, 0))
    row = pl.BlockSpec((None, 1, d), lambda b, i: (b, 0, 0))
    tab = pl.BlockSpec((None, tm, LANES), lambda b, i: (b, i, 0))
    shape = jax.ShapeDtypeStruct((bsz, seq, d), BF16)
    return pl.pallas_call(
        _qkv1_kernel,
        grid=(bsz, seq // tm),
        in_specs=[x_spec, _resident((1, d)), row, row, _resident((1, d)), row, row,
                  _resident((d, d)), _resident((d, d)), _resident((d, d)), tab, tab, tab],
        out_specs=[x_spec, x_spec, pl.BlockSpec((None, None, d, tm), lambda b, i: (b, i, 0, 0))],
        out_shape=[shape, shape, jax.ShapeDtypeStruct((bsz, seq // tm, d, tm), BF16)],
        compiler_params=_params(2),
        name="qkv1",
    )(x, gq.reshape(1, d), scq, shq, gkv.reshape(1, d), sckv, shkv, wq, wk, wvt, *tables)


def _diff_kernel(lq1_ref, lk1_ref, lq2_ref, lk2_ref, g_ref, q_ref, k_ref, vt_ref, o_ref,
                 *scratch, tq, lam_init):
    lam = (jnp.exp(jnp.sum(lq1_ref[...] * lk1_ref[...], axis=-1, keepdims=True))
           - jnp.exp(jnp.sum(lq2_ref[...] * lk2_ref[...], axis=-1, keepdims=True)) + lam_init)
    scale = g_ref[...] * (1.0 - lam_init)

    def tile(i, carry):
        _diff_tile(i, lam, scale, q_ref, k_ref, vt_ref, o_ref, *scratch, tq=tq)
        return carry

    lax.fori_loop(0, q_ref.shape[0] // tq, tile, 0)


def _diff_tile(i, lam, scale, q_ref, k_ref, vt_ref, o_ref, acc_ref, m_ref, s_ref, bmax_ref, *, tq):
    rows = pl.ds(pl.multiple_of(i * tq, tq), tq)
    q = q_ref[rows, :]
    lane = lax.broadcasted_iota(jnp.int32, q.shape, 1)
    zero = jnp.zeros_like(q)
    q_maps = (jnp.where(lane < HEAD_DIM, q, zero), jnp.where(lane >= HEAD_DIM, q, zero))
    acc_ref[...] = jnp.zeros(acc_ref.shape, F32)
    m_ref[...] = jnp.full(m_ref.shape, NEG_INF, F32)

    half = tq // 2
    full = (0, tq, 0, tq)
    diag_a = (0, half, 0, tq)
    diag_b = (half, half, half, half)

    def scores(jb, slot, part, masked):
        k0, nk, q0, nq = part
        start = jb * tq + k0
        if not isinstance(jb, int):
            start = pl.multiple_of(start, half)
        k = k_ref[pl.ds(start, nk), :]
        for c in range(2):
            s = lax.dot_general(k, q_maps[c][q0:q0 + nq, :], (((1,), (1,)), ((), ())),
                                preferred_element_type=F32)
            if masked:
                key = lax.broadcasted_iota(jnp.int32, (nk, nq), 0) + k0
                qry = lax.broadcasted_iota(jnp.int32, (nk, nq), 1) + q0
                s = jnp.where(key <= qry, s, NEG_INF)
            s_ref[slot, c, 0:nk, q0:q0 + nq] = s
            bmax_ref[slot, c, :, q0:q0 + nq] = jnp.max(s, axis=0, keepdims=True)

    def update(jb, slot, part):
        k0, nk, q0, nq = part
        lanes = slice(q0, q0 + nq)
        vt = jnp.concatenate([vt_ref[jb, :, k0:k0 + nk], jnp.ones((ONES_ROWS, nk), BF16)], axis=0)
        for c in range(2):
            s = s_ref[slot, c, 0:nk, lanes]
            m_old = m_ref[c, :, lanes]
            m_new = jnp.maximum(m_old, bmax_ref[slot, c, :, lanes])
            alpha = jnp.exp2(m_old - m_new)
            p = jnp.exp2(s - m_new)
            acc_ref[c, :, lanes] = (alpha * acc_ref[c, :, lanes]
                                    + jnp.dot(vt, p.astype(BF16), preferred_element_type=F32))
            m_ref[c, :, lanes] = m_new

    def diagonal(prev_slot):
        a_slot = 0 if prev_slot is None else 1 - prev_slot
        scores(i, a_slot, diag_a, True)
        if prev_slot is not None:
            update(i - 1, prev_slot, full)
        scores(i, 1 - a_slot, diag_b, True)
        update(i, a_slot, diag_a)
        update(i, 1 - a_slot, diag_b)

    odd = (i & 1) == 1

    @pl.when(i == 0)
    def _():
        diagonal(None)

    @pl.when(i > 0)
    def _():
        scores(0, 0, full, False)

    def body(t, carry):
        for slot in range(2):
            scores(2 * t + slot + 1, 1 - slot, full, False)
            update(2 * t + slot, slot, full)
        return carry

    lax.fori_loop(0, lax.shift_right_arithmetic(i - 1, 1), body, 0)

    @pl.when(odd)
    def _():
        diagonal(0)

    @pl.when(jnp.logical_and(i > 0, jnp.logical_not(odd)))
    def _():
        scores(i - 1, 1, full, False)
        update(i - 2, 0, full)
        diagonal(1)

    width = 2 * HEAD_DIM
    outs = [acc_ref[c, 0:width, :] * (1.0 / acc_ref[c, width:width + 1, :]) for c in range(2)]
    o_t = outs[0] - lam * outs[1]
    ms = jnp.mean(o_t * o_t, axis=0, keepdims=True)
    o_t = o_t * lax.rsqrt(ms + SUBLN_EPS) * scale
    o_ref[rows, :] = o_t.astype(BF16).T


def _diff_attention(q, k, vt, lq1, lk1, lq2, lk2, subln, lam_init, tq):
    bsz, seq, d = q.shape
    width = 2 * HEAD_DIM
    vec = _resident((1, HEAD_DIM))
    head_spec = pl.BlockSpec((None, seq, width), lambda b, h: (b, 0, h))
    vt_spec = pl.BlockSpec((None, seq // tq, width, tq), lambda b, h: (b, 0, h, 0))
    return pl.pallas_call(
        functools.partial(_diff_kernel, tq=tq, lam_init=lam_init),
        grid=(bsz, B_HEADS),
        in_specs=[vec, vec, vec, vec, _resident((width, 1)), head_spec, head_spec, vt_spec],
        out_specs=head_spec,
        out_shape=jax.ShapeDtypeStruct((bsz, seq, d), BF16),
        scratch_shapes=[pltpu.VMEM((2, width + ONES_ROWS, tq), F32), pltpu.VMEM((2, 1, tq), F32),
                        pltpu.VMEM((2, 2, tq, tq), F32),
                        pltpu.VMEM((2, 2, 1, tq), F32)],
        compiler_params=_params(2),
        name="diff_attn",
    )(lq1.reshape(1, HEAD_DIM), lk1.reshape(1, HEAD_DIM), lq2.reshape(1, HEAD_DIM),
      lk2.reshape(1, HEAD_DIM), subln.reshape(width, 1), q, k, vt)


def _split3(m):
    return jnp.split(m[:, None, :], 3, axis=-1)


def kernel(x, c, positions, mod_mix_w, mod_mix_b, mod_ffn_w, mod_ffn_b, norm_pre_mix, norm_post_mix, norm_pre_ffn, norm_post_ffn, ffn_w_gate, ffn_w_up, ffn_conv_w, ffn_conv_b, ffn_w_down, a_w_qkv, a_w_o, kv_norm, kv_mod_w, kv_mod_b, b_w_k, b_w_v, b_w_q, b_lambda_q1, b_lambda_k1, b_lambda_q2, b_lambda_k2, b_subln, b_w_o):
    mix_mod = _ada_mod(c, mod_mix_w, mod_mix_b)
    ffn_mod = _ada_mod(c, mod_ffn_w, mod_ffn_b)
    kv_mod = _ada_mod(c, kv_mod_w[None], kv_mod_b[None])[0]
    tables = _rope_tables(positions)

    def tail(attn, x, w_o, gate_mix, l):
        shift, scale, gate = _split3(ffn_mod[l])
        return _layer_tail(attn, x, w_o.astype(BF16), norm_post_mix[l], gate_mix,
                           norm_pre_ffn[l], scale, shift, gate, ffn_w_gate[l].astype(BF16),
                           ffn_w_up[l].astype(BF16), ffn_conv_w[l], ffn_conv_b[l],
                           (0.5 * ffn_w_down[l]).astype(BF16), norm_post_ffn[l])

    shift, scale, gate = _split3(mix_mod[0])
    qkv_groups = _qkv0(x, norm_pre_mix[0], scale, shift, a_w_qkv[0].astype(BF16), tables)
    outs, lses = [], []
    for qkv, (_, dilation), (tq, unroll) in zip(qkv_groups, A_GROUPS, DIL_TILES):
        o, lse = _dilated_attention(qkv, dilation, tq, unroll)
        outs.append(o)
        lses.append(lse)
    x = tail((outs, lses), x, a_w_o[0], gate, 0)

    shift, scale, gate = _split3(mix_mod[1])
    kv_shift, kv_scale = jnp.split(kv_mod[:, None, :], 2, axis=-1)
    q, k, vt = _qkv1(x, norm_pre_mix[1], scale, shift, kv_norm, kv_scale, kv_shift,
                     b_w_q[0].astype(BF16), b_w_k.astype(BF16), b_w_v.T.astype(BF16), tables,
                     tm=DIFF_TILE)
    lam_init = 0.8 - 0.6 * math.exp(-0.3 * 1)
    o = _diff_attention(q, k, vt, b_lambda_q1[0], b_lambda_k1[0], b_lambda_q2[0], b_lambda_k2[0],
                        b_subln[0], lam_init, tq=DIFF_TILE)
    x = tail(o, x, b_w_o[0], gate, 1)
    return x
```
